```python
import jax, jax.numpy as jnp
from jax import lax
import numpy as np

D_MODEL = 1024
BATCH = 4
SEQ = 4096
DEPTH = 1
DEC_BATCH = 16
DEC_SEQ = 2048
PAST_LEN = 128

N_META = 16
D_MIX = D_MODEL
D_CONV = D_MIX // 2
CONV_WIDTH = 3
N_HEADS = 8
QK_NOPE = 64
QK_ROPE = 32
V_HEAD = 64
Q_LORA = (3 * D_MODEL) // 8
KV_LORA = D_MODEL // 4
D_FF = 2816
Q_BLOCK = 128
ROPE_BASE = 10000.0
EPS = 1e-6
D_IN = 3 * D_CONV + Q_LORA + KV_LORA + QK_ROPE

kernel_name = "hybrid_conv_mla_macaron_encoder"


def rms_norm(x, g):
    xf = x.astype(jnp.float32)
    y = xf * lax.rsqrt(jnp.mean(xf * xf, axis=-1, keepdims=True) + EPS)
    return (y * g.astype(jnp.float32)).astype(x.dtype)


def swiglu(x, w_gate, w_up, w_down):
    return (jax.nn.silu(x @ w_gate) * (x @ w_up)) @ w_down


def rope_tables(seq_len):
    pos = jnp.arange(seq_len, dtype=jnp.float32)
    inv_freq = 1.0 / (ROPE_BASE ** (jnp.arange(0, QK_ROPE, 2, dtype=jnp.float32) / QK_ROPE))
    ang = pos[:, None] * inv_freq[None, :]
    return jnp.cos(ang), jnp.sin(ang)


def apply_rope(x, cos, sin):
    xf = x.astype(jnp.float32)
    half = QK_ROPE // 2
    x1, x2 = xf[..., :half], xf[..., half:]
    shape = (1, cos.shape[0]) + (1,) * (x.ndim - 3) + (half,)
    c = cos.reshape(shape)
    s = sin.reshape(shape)
    return jnp.concatenate([x1 * c - x2 * s, x2 * c + x1 * s], axis=-1).astype(x.dtype)


def short_conv_mixer(b_gate, c_gate, h, conv_w):
    u = c_gate * h
    up = jnp.pad(u, ((0, 0), (1, 1), (0, 0)))
    y = up[:, :-2] * conv_w[0] + up[:, 1:-1] * conv_w[1] + up[:, 2:] * conv_w[2]
    return b_gate * y


def latent_attention(q_lat, kv_lat, k_rope_raw, cos, sin, q_norm, w_uq, kv_norm, w_ukv):
    bsz, seq_len, _ = q_lat.shape
    q = (rms_norm(q_lat, q_norm) @ w_uq).reshape(bsz, seq_len, N_HEADS, QK_NOPE + QK_ROPE)
    q_nope = q[..., :QK_NOPE]
    q_rope = apply_rope(q[..., QK_NOPE:], cos, sin)
    kv = (rms_norm(kv_lat, kv_norm) @ w_ukv).reshape(bsz, seq_len, N_HEADS, QK_NOPE + V_HEAD)
    k_nope = kv[..., :QK_NOPE]
    v = kv[..., QK_NOPE:]
    k_rope = apply_rope(k_rope_raw, cos, sin)

    n_blk = -(-seq_len // Q_BLOCK)
    pad = n_blk * Q_BLOCK - seq_len
    scale = (QK_NOPE + QK_ROPE) ** -0.5

    def to_blocks(t):
        t = jnp.pad(t, ((0, 0), (0, pad), (0, 0), (0, 0)))
        return jnp.moveaxis(t.reshape(bsz, n_blk, Q_BLOCK, N_HEADS, t.shape[-1]), 1, 0)

    def attend(blk):
        qn, qr = blk
        s = (jnp.einsum('bqhd,bkhd->bhqk', qn, k_nope, preferred_element_type=jnp.float32)
             + jnp.einsum('bqhd,bkd->bhqk', qr, k_rope, preferred_element_type=jnp.float32))
        p = jax.nn.softmax(s * scale, axis=-1).astype(v.dtype)
        return jnp.einsum('bhqk,bkhd->bqhd', p, v)

    o = lax.map(attend, (to_blocks(q_nope), to_blocks(q_rope)))
    o = jnp.moveaxis(o, 0, 1).reshape(bsz, n_blk * Q_BLOCK, N_HEADS, V_HEAD)[:, :seq_len]
    return o.reshape(bsz, seq_len, N_HEADS * V_HEAD)


def encoder_layer(x, cos, sin, ffn1_norm, ffn1_w_gate, ffn1_w_up, ffn1_w_down,
                  mix_norm, w_in, conv_w, q_norm, w_uq, kv_norm, w_ukv, w_out,
                  ffn2_norm, ffn2_w_gate, ffn2_w_up, ffn2_w_down):
    x = x + 0.5 * swiglu(rms_norm(x, ffn1_norm), ffn1_w_gate, ffn1_w_up, ffn1_w_down)
    z = rms_norm(x, mix_norm) @ w_in
    cuts = [D_CONV, 2 * D_CONV, 3 * D_CONV, 3 * D_CONV + Q_LORA, 3 * D_CONV + Q_LORA + KV_LORA]
    b_gate, c_gate, h, q_lat, kv_lat, k_rope_raw = jnp.split(z, cuts, axis=-1)
    y_conv = short_conv_mixer(b_gate, c_gate, h, conv_w)
    y_att = latent_attention(q_lat, kv_lat, k_rope_raw, cos, sin, q_norm, w_uq, kv_norm, w_ukv)
    x = x + jnp.concatenate([y_conv, y_att], axis=-1) @ w_out
    x = x + 0.5 * swiglu(rms_norm(x, ffn2_norm), ffn2_w_gate, ffn2_w_up, ffn2_w_down)
    return x


def trunk(x, meta_tokens, final_norm, layer_params):
    bsz = x.shape[0]
    meta = jnp.broadcast_to(meta_tokens.astype(x.dtype)[None], (bsz, N_META, D_MODEL))
    h = jnp.concatenate([meta, x], axis=1)
    cos, sin = rope_tables(h.shape[1])
    for l in range(DEPTH):
        h = encoder_layer(h, cos, sin, *[p[l] for p in layer_params])
    h = rms_norm(h, final_norm)
    return h[:, N_META:]


def setup_inputs(seed: int = 0) -> dict:
    key = jax.random.key(seed)
    ks = jax.random.split(key, 24)
    f32 = jnp.float32

    def w(k, shape, fan_in):
        return jax.random.normal(k, shape, f32) * (fan_in ** -0.5)

    def gain(k, shape):
        return 1.0 + 0.02 * jax.random.normal(k, shape, f32)

    return {
        "x_prompt": jax.random.normal(ks[0], (BATCH, SEQ, D_MODEL), f32),
        "x_sample": jax.random.normal(ks[1], (DEC_BATCH, DEC_SEQ, D_MODEL), f32),
        "meta_tokens": jax.random.normal(ks[2], (N_META, D_MODEL), f32),
        "ffn1_norm": gain(ks[3], (DEPTH, D_MODEL)),
        "ffn1_w_gate": w(ks[4], (DEPTH, D_MODEL, D_FF), D_MODEL),
        "ffn1_w_up": w(ks[5], (DEPTH, D_MODEL, D_FF), D_MODEL),
        "ffn1_w_down": w(ks[6], (DEPTH, D_FF, D_MODEL), D_FF),
        "mix_norm": gain(ks[7], (DEPTH, D_MODEL)),
        "w_in": w(ks[8], (DEPTH, D_MODEL, D_IN), D_MODEL),
        "conv_w": w(ks[9], (DEPTH, CONV_WIDTH, D_CONV), CONV_WIDTH),
        "q_norm": gain(ks[10], (DEPTH, Q_LORA)),
        "w_uq": w(ks[11], (DEPTH, Q_LORA, N_HEADS * (QK_NOPE + QK_ROPE)), Q_LORA),
        "kv_norm": gain(ks[12], (DEPTH, KV_LORA)),
        "w_ukv": w(ks[13], (DEPTH, KV_LORA, N_HEADS * (QK_NOPE + V_HEAD)), KV_LORA),
        "w_out": w(ks[14], (DEPTH, D_MIX, D_MODEL), D_MIX),
        "ffn2_norm": gain(ks[15], (DEPTH, D_MODEL)),
        "ffn2_w_gate": w(ks[16], (DEPTH, D_MODEL, D_FF), D_MODEL),
        "ffn2_w_up": w(ks[17], (DEPTH, D_MODEL, D_FF), D_MODEL),
        "ffn2_w_down": w(ks[18], (DEPTH, D_FF, D_MODEL), D_FF),
        "final_norm": gain(ks[19], (D_MODEL,)),
    }


def reference(x_prompt, x_sample, meta_tokens, ffn1_norm, ffn1_w_gate, ffn1_w_up, ffn1_w_down,
              mix_norm, w_in, conv_w, q_norm, w_uq, kv_norm, w_ukv, w_out,
              ffn2_norm, ffn2_w_gate, ffn2_w_up, ffn2_w_down, final_norm):
    layer_params = (ffn1_norm, ffn1_w_gate, ffn1_w_up, ffn1_w_down,
                    mix_norm, w_in, conv_w, q_norm, w_uq, kv_norm, w_ukv, w_out,
                    ffn2_norm, ffn2_w_gate, ffn2_w_up, ffn2_w_down)
    y_prompt = trunk(x_prompt, meta_tokens, final_norm, layer_params)
    y_sample = trunk(x_sample, meta_tokens, final_norm, layer_params)
    return (y_prompt, y_sample)
```

```python
import functools

import jax
import jax.numpy as jnp
from jax import lax
from jax.experimental import pallas as pl
from jax.experimental.pallas import tpu as pltpu

D_MODEL = 1024
N_META = 16
D_CONV = 512
N_HEADS = 8
QK_NOPE = 64
QK_ROPE = 32
V_HEAD = 64
Q_LORA = 384
KV_LORA = 256
D_FF = 2816
ROPE_BASE = 10000.0
EPS = 1e-6

HEAD_PAD = 128
D_QK = N_HEADS * HEAD_PAD
D_V = N_HEADS * V_HEAD
D_Z = 3 * D_CONV + Q_LORA + KV_LORA + 2 * HEAD_PAD
FF_CHUNK = 256
HALO = 8
VMEM_LIMIT = 56 * 1024 * 1024

BF16 = jnp.bfloat16
F32 = jnp.float32
_NT = (((1,), (1,)), ((), ()))


def _dot(a, b):
    return jnp.dot(a, b, preferred_element_type=F32)


def _rms(x, g):
    ms = jnp.mean(x * x, axis=-1, keepdims=True)
    return x * lax.rsqrt(ms + EPS) * g


def _swiglu_half_step(x, g_ref, wg_ref, wu_ref, wd_ref, h_ref):
    xn = _rms(x, g_ref[...]).astype(BF16)
    for c in range(D_FF // FF_CHUNK):
        cols = slice(c * FF_CHUNK, (c + 1) * FF_CHUNK)
        gate = _dot(xn, wg_ref[:, cols])
        up = _dot(xn, wu_ref[:, cols])
        h_ref[:, cols] = (gate * jax.nn.sigmoid(gate) * up).astype(BF16)
    return x + 0.5 * _dot(h_ref[...], wd_ref[...])


def _ffn1_kernel(x_ref, g_ref, wg_ref, wu_ref, wd_ref, o_ref, h_ref):
    o_ref[...] = _swiglu_half_step(x_ref[...], g_ref, wg_ref, wu_ref, wd_ref, h_ref)


def _out_ffn2_kernel(x_ref, yc_ref, ya_ref, wo_ref, g_ref, wg_ref, wu_ref, wd_ref, gf_ref, o_ref, h_ref):
    x = x_ref[...] + _dot(yc_ref[...], wo_ref[:D_CONV, :]) + _dot(ya_ref[...], wo_ref[D_CONV:, :])
    x = _swiglu_half_step(x, g_ref, wg_ref, wu_ref, wd_ref, h_ref)
    o_ref[...] = _rms(x, gf_ref[...])


def _mix_in_kernel(x_ref, xl_ref, xr_ref, xm_ref, cos_ref, sin_ref, g_ref, win_ref, cw_ref,
                   gq_ref, wuq_ref, gkv_ref, wukv_ref, yc_ref, q_ref, k_ref, v_ref, *, scale):
    t = pl.program_id(1)
    n_t = pl.num_programs(1)
    rows = x_ref.shape[0]
    g = g_ref[...]
    z = _dot(_rms(x_ref[...], g).astype(BF16), win_ref[...])
    b_gate = z[:, 0:D_CONV]
    u = z[:, D_CONV:2 * D_CONV] * z[:, 2 * D_CONV:3 * D_CONV]
    o_q = 3 * D_CONV
    o_kv = o_q + Q_LORA
    o_kr = o_kv + KV_LORA

    left = jnp.where(t == 0, xm_ref[N_META - HALO:, :], xl_ref[...])
    xh = jnp.concatenate([left, xr_ref[...]], axis=0)
    zh = _dot(_rms(xh, g).astype(BF16), win_ref[:, D_CONV:3 * D_CONV])
    uh = zh[:, :D_CONV] * zh[:, D_CONV:]
    u_before = uh[HALO - 1:HALO, :]
    u_after = jnp.where(t == n_t - 1, 0.0, uh[HALO:HALO + 1, :])
    row = lax.broadcasted_iota(jnp.int32, (rows, 1), 0)
    u_prev = jnp.where(row == 0, u_before, pltpu.roll(u, 1, 0))
    u_next = jnp.where(row == rows - 1, u_after, pltpu.roll(u, rows - 1, 0))
    cw = cw_ref[...]
    yc_ref[...] = (b_gate * (u_prev * cw[0:1, :] + u * cw[1:2, :] + u_next * cw[2:3, :])).astype(BF16)

    cos = cos_ref[...]
    sin = sin_ref[...]
    qq = _dot(_rms(z[:, o_q:o_kv], gq_ref[...]).astype(BF16), wuq_ref[...])
    kv = _dot(_rms(z[:, o_kv:o_kr], gkv_ref[...]).astype(BF16), wukv_ref[...])
    k_rope = z[:, o_kr:o_kr + HEAD_PAD] * cos + z[:, o_kr + HEAD_PAD:] * sin
    for h in range(N_HEADS):
        cols = slice(h * HEAD_PAD, (h + 1) * HEAD_PAD)
        rot = slice(D_QK + h * HEAD_PAD, D_QK + (h + 1) * HEAD_PAD)
        q_ref[:, cols] = ((qq[:, cols] * cos + qq[:, rot] * sin) * scale).astype(BF16)
        k_ref[:, cols] = (kv[:, cols] + k_rope).astype(BF16)
    v_ref[...] = kv[:, D_QK:].astype(BF16)


def _attention_kernel(q_ref, k_ref, v_ref, km_ref, vm_ref, o_ref, *, kv_block):
    n_kv = k_ref.shape[0] // kv_block
    rows = q_ref.shape[0]
    lane = lax.broadcasted_iota(jnp.int32, (rows, HEAD_PAD), 1)
    for pair in range(N_HEADS // 2):
        vcols = slice(pair * HEAD_PAD, (pair + 1) * HEAD_PAD)
        outs = []
        for h in (2 * pair, 2 * pair + 1):
            cols = slice(h * HEAD_PAD, (h + 1) * HEAD_PAD)
            qh = q_ref[:, cols]
            s = lax.dot_general(qh, km_ref[:, cols], _NT, preferred_element_type=F32)
            m = jnp.max(s, axis=1, keepdims=True)
            p = jnp.exp(s - m)
            l = jnp.sum(p, axis=1, keepdims=True)
            acc = _dot(p.astype(BF16), vm_ref[:, vcols])

            def body(c, carry, qh=qh, cols=cols, vcols=vcols):
                m, l, acc = carry
                off = pl.multiple_of(c * kv_block, kv_block)
                s = lax.dot_general(qh, k_ref[pl.ds(off, kv_block), cols], _NT, preferred_element_type=F32)
                m_new = jnp.maximum(m, jnp.max(s, axis=1, keepdims=True))
                alpha = jnp.exp(m - m_new)
                p = jnp.exp(s - m_new)
                l = alpha * l + jnp.sum(p, axis=1, keepdims=True)
                acc = alpha * acc + _dot(p.astype(BF16), v_ref[pl.ds(off, kv_block), vcols])
                return m_new, l, acc

            m, l, acc = lax.fori_loop(0, n_kv, body, (m, l, acc))
            outs.append(acc / l)
        o_ref[:, vcols] = jnp.where(lane < V_HEAD, outs[0], outs[1]).astype(BF16)


def _const_spec(shape):
    return pl.BlockSpec(shape, lambda *_: (0,) * len(shape), pipeline_mode=pl.Buffered(1))


def _params(n_grid):
    return pltpu.CompilerParams(dimension_semantics=("arbitrary",) * n_grid, vmem_limit_bytes=VMEM_LIMIT)


def _ffn1(x, g, wg, wu, wd, rows):
    n = x.shape[0]
    return pl.pallas_call(
        _ffn1_kernel,
        grid=(n // rows,),
        in_specs=[pl.BlockSpec((rows, D_MODEL), lambda i: (i, 0)),
                  _const_spec((1, D_MODEL)), _const_spec((D_MODEL, D_FF)), _const_spec((D_MODEL, D_FF)),
                  _const_spec((D_FF, D_MODEL))],
        out_specs=pl.BlockSpec((rows, D_MODEL), lambda i: (i, 0)),
        out_shape=jax.ShapeDtypeStruct((n, D_MODEL), F32),
        scratch_shapes=[pltpu.VMEM((rows, D_FF), BF16)],
        compiler_params=_params(1),
        name="ffn1",
    )(x, g, wg, wu, wd)


def _out_ffn2(x, yc, ya, wo, g, wg, wu, wd, gf, rows):
    n = x.shape[0]
    return pl.pallas_call(
        _out_ffn2_kernel,
        grid=(n // rows,),
        in_specs=[pl.BlockSpec((rows, D_MODEL), lambda i: (i, 0)),
                  pl.BlockSpec((rows, D_CONV), lambda i: (i, 0)),
                  pl.BlockSpec((rows, D_V), lambda i: (i, 0)),
                  _const_spec((D_CONV + D_V, D_MODEL)),
                  _const_spec((1, D_MODEL)), _const_spec((D_MODEL, D_FF)), _const_spec((D_MODEL, D_FF)),
                  _const_spec((D_FF, D_MODEL)), _const_spec((1, D_MODEL))],
        out_specs=pl.BlockSpec((rows, D_MODEL), lambda i: (i, 0)),
        out_shape=jax.ShapeDtypeStruct((n, D_MODEL), F32),
        scratch_shapes=[pltpu.VMEM((rows, D_FF), BF16)],
        compiler_params=_params(1),
        name="out_ffn2",
    )(x, yc, ya, wo, g, wg, wu, wd, gf)


def _mix_in(x, x_meta, cos, sin, g, win, cw, gq, wuq, gkv, wukv, rows, scale):
    bsz, seq, _ = x.shape
    n_t = seq // rows
    per = rows // HALO
    last = seq // HALO - 1
    tile = lambda d: pl.BlockSpec((None, rows, d), lambda b, t: (b, t, 0))
    return pl.pallas_call(
        functools.partial(_mix_in_kernel, scale=scale),
        grid=(bsz, n_t),
        in_specs=[tile(D_MODEL),
                  pl.BlockSpec((None, HALO, D_MODEL), lambda b, t: (b, jnp.maximum(t * per - 1, 0), 0)),
                  pl.BlockSpec((None, HALO, D_MODEL), lambda b, t: (b, jnp.minimum((t + 1) * per, last), 0)),
                  _const_spec((N_META, D_MODEL)),
                  pl.BlockSpec((rows, HEAD_PAD), lambda b, t: (t, 0)),
                  pl.BlockSpec((rows, HEAD_PAD), lambda b, t: (t, 0)),
                  _const_spec((1, D_MODEL)), _const_spec((D_MODEL, D_Z)), _const_spec((3, D_CONV)),
                  _const_spec((1, Q_LORA)), _const_spec((Q_LORA, 2 * D_QK)),
                  _const_spec((1, KV_LORA)), _const_spec((KV_LORA, D_QK + D_V))],
        out_specs=[tile(D_CONV), tile(D_QK), tile(D_QK), tile(D_V)],
        out_shape=[jax.ShapeDtypeStruct((bsz, seq, D_CONV), BF16),
                   jax.ShapeDtypeStruct((bsz, seq, D_QK), BF16),
                   jax.ShapeDtypeStruct((bsz, seq, D_QK), BF16),
                   jax.ShapeDtypeStruct((bsz, seq, D_V), BF16)],
        compiler_params=_params(2),
        name="mix_in",
    )(x, x, x, x_meta, cos, sin, g, win, cw, gq, wuq, gkv, wukv)


def _attention(q, k, v, k_meta, v_meta, q_block, kv_block):
    bsz, seq, _ = q.shape
    return pl.pallas_call(
        functools.partial(_attention_kernel, kv_block=kv_block),
        grid=(bsz, seq // q_block),
        in_specs=[pl.BlockSpec((None, q_block, D_QK), lambda b, t: (b, t, 0)),
                  pl.BlockSpec((None, seq, D_QK), lambda b, t: (b, 0, 0)),
                  pl.BlockSpec((None, seq, D_V), lambda b, t: (b, 0, 0)),
                  _const_spec((N_META, D_QK)), _const_spec((N_META, D_V))],
        out_specs=pl.BlockSpec((None, q_block, D_V), lambda b, t: (b, t, 0)),
        out_shape=jax.ShapeDtypeStruct((bsz, seq, D_V), BF16),
        compiler_params=_params(2),
        name="attention",
    )(q, k, v, k_meta, v_meta)


def _rot_half(w):
    half = QK_ROPE // 2
    return jnp.concatenate([-w[..., half:], w[..., :half]], axis=-1)


def _prepare_weights(w_in, w_uq, w_ukv):
    w_kr = w_in[:, 3 * D_CONV + Q_LORA + KV_LORA:]
    pad_l = jnp.zeros((D_MODEL, QK_NOPE), F32)
    pad_r = jnp.zeros((D_MODEL, HEAD_PAD - QK_NOPE - QK_ROPE), F32)
    win = jnp.concatenate([w_in[:, :3 * D_CONV + Q_LORA + KV_LORA],
                           pad_l, w_kr, pad_r, pad_l, _rot_half(w_kr), pad_r], axis=1)
    wq = w_uq.reshape(Q_LORA, N_HEADS, QK_NOPE + QK_ROPE)
    zq = jnp.zeros((Q_LORA, N_HEADS, HEAD_PAD - QK_NOPE - QK_ROPE), F32)
    q_base = jnp.concatenate([wq, zq], axis=-1).reshape(Q_LORA, D_QK)
    q_rot = jnp.concatenate([jnp.zeros((Q_LORA, N_HEADS, QK_NOPE), F32), _rot_half(wq[..., QK_NOPE:]), zq],
                            axis=-1).reshape(Q_LORA, D_QK)
    wuq = jnp.concatenate([q_base, q_rot], axis=1)
    wkv = w_ukv.reshape(KV_LORA, N_HEADS, QK_NOPE + V_HEAD)
    k_part = jnp.concatenate([wkv[..., :QK_NOPE], jnp.zeros((KV_LORA, N_HEADS, HEAD_PAD - QK_NOPE), F32)],
                             axis=-1).reshape(KV_LORA, D_QK)
    wukv = jnp.concatenate([k_part, wkv[..., QK_NOPE:].reshape(KV_LORA, D_V)], axis=1)
    return win.astype(BF16), wuq.astype(BF16), wukv.astype(BF16)


def _rope_tables(n_pos):
    pos = jnp.arange(n_pos, dtype=F32)
    inv_freq = 1.0 / (ROPE_BASE ** (jnp.arange(0, QK_ROPE, 2, dtype=F32) / QK_ROPE))
    ang = pos[:, None] * inv_freq[None, :]
    cos, sin = jnp.cos(ang), jnp.sin(ang)
    ones = jnp.ones((n_pos, QK_NOPE), F32)
    zeros = jnp.zeros((n_pos, QK_NOPE), F32)
    pad = jnp.zeros((n_pos, HEAD_PAD - QK_NOPE - QK_ROPE), F32)
    return (jnp.concatenate([ones, cos, cos, pad], axis=1), jnp.concatenate([zeros, sin, sin, pad], axis=1))


def _pick(n, target):
    while n % target:
        target //= 2
    return target


def kernel(x_prompt, x_sample, meta_tokens, ffn1_norm, ffn1_w_gate, ffn1_w_up, ffn1_w_down, mix_norm, w_in, conv_w, q_norm, w_uq, kv_norm, w_ukv, w_out, ffn2_norm, ffn2_w_gate, ffn2_w_up, ffn2_w_down, final_norm):
    assert ffn1_norm.shape[0] == 1, "single layer: meta-token outputs are only consumed as keys/values"
    scale = (QK_NOPE + QK_ROPE) ** -0.5
    g1, g2, gm = ffn1_norm, ffn2_norm, mix_norm
    gq, gkv, gf = q_norm, kv_norm, final_norm.reshape(1, D_MODEL)
    wg1, wu1, wd1 = ffn1_w_gate[0].astype(BF16), ffn1_w_up[0].astype(BF16), ffn1_w_down[0].astype(BF16)
    wg2, wu2, wd2 = ffn2_w_gate[0].astype(BF16), ffn2_w_up[0].astype(BF16), ffn2_w_down[0].astype(BF16)
    wo = w_out[0].astype(BF16)
    win, wuq, wukv = _prepare_weights(w_in[0], w_uq[0], w_ukv[0])
    cw = conv_w[0]

    max_seq = max(x_prompt.shape[1], x_sample.shape[1])
    cos, sin = _rope_tables(N_META + max_seq)

    xm1 = _ffn1(meta_tokens, g1, wg1, wu1, wd1, N_META)
    _, _, k_meta, v_meta = _mix_in(xm1[None], xm1, cos[:N_META], sin[:N_META], gm, win, cw, gq, wuq, gkv, wukv,
                                   N_META, scale)
    k_meta, v_meta = k_meta[0], v_meta[0]

    def trunk(x):
        bsz, seq, _ = x.shape
        rows = _pick(seq, 512)
        x1 = _ffn1(x.reshape(bsz * seq, D_MODEL), g1, wg1, wu1, wd1, rows)
        yc, q, k, v = _mix_in(x1.reshape(bsz, seq, D_MODEL), xm1, cos[N_META:N_META + seq], sin[N_META:N_META + seq],
                              gm, win, cw, gq, wuq, gkv, wukv, rows, scale)
        ya = _attention(q, k, v, k_meta, v_meta, _pick(seq, 512), _pick(seq, 512))
        y = _out_ffn2(x1, yc.reshape(bsz * seq, D_CONV), ya.reshape(bsz * seq, D_V), wo, g2, wg2, wu2, wd2, gf, rows)
        return y.reshape(bsz, seq, D_MODEL)

    return (trunk(x_prompt), trunk(x_sample))
```

```python
import functools

import jax
import jax.numpy as jnp
from jax import lax
from jax.experimental import pallas as pl
from jax.experimental.pallas import tpu as pltpu

D_MODEL = 1024
N_META = 16
D_CONV = 512
N_HEADS = 8
QK_NOPE = 64
QK_ROPE = 32
V_HEAD = 64
Q_LORA = 384
KV_LORA = 256
D_FF = 2816
ROPE_BASE = 10000.0
EPS = 1e-6
LOG2_E = 1.4426950408889634

HEAD_PAD = 128
D_QK = N_HEADS * HEAD_PAD
D_V = N_HEADS * V_HEAD
D_Z = 3 * D_CONV + Q_LORA + KV_LORA + 2 * HEAD_PAD
FF_CHUNK = 256
HALO = 8
VMEM_LIMIT = 56 * 1024 * 1024
ATTN_LOOKAHEAD = 2

BF16 = jnp.bfloat16
F32 = jnp.float32
_NT = (((1,), (1,)), ((), ()))


def _dot(a, b):
    return jnp.dot(a, b, preferred_element_type=F32)


def _rms(x, g):
    ms = jnp.mean(x * x, axis=-1, keepdims=True)
    return x * lax.rsqrt(ms + EPS) * g


def _swiglu_half_step(x, g_ref, wg_ref, wu_ref, wd_ref, h_ref):
    xn = _rms(x, g_ref[...]).astype(BF16)
    for c in range(D_FF // FF_CHUNK):
        cols = slice(c * FF_CHUNK, (c + 1) * FF_CHUNK)
        gate = _dot(xn, wg_ref[:, cols])
        up = _dot(xn, wu_ref[:, cols])
        h_ref[:, cols] = (gate * jax.nn.sigmoid(gate) * up).astype(BF16)
    return x + 0.5 * _dot(h_ref[...], wd_ref[...])


def _ffn1_kernel(x_ref, g_ref, wg_ref, wu_ref, wd_ref, o_ref, h_ref):
    o_ref[...] = _swiglu_half_step(x_ref[...], g_ref, wg_ref, wu_ref, wd_ref, h_ref)


def _out_ffn2_kernel(x_ref, yc_ref, ya_ref, wo_ref, g_ref, wg_ref, wu_ref, wd_ref, gf_ref, o_ref, h_ref):
    x = x_ref[...] + _dot(yc_ref[...], wo_ref[:D_CONV, :]) + _dot(ya_ref[...], wo_ref[D_CONV:, :])
    x = _swiglu_half_step(x, g_ref, wg_ref, wu_ref, wd_ref, h_ref)
    o_ref[...] = _rms(x, gf_ref[...])


def _mix_in_kernel(x_ref, xl_ref, xr_ref, xm_ref, cos_ref, sin_ref, g_ref, win_ref, cw_ref,
                   gq_ref, wuq_ref, gkv_ref, wuk_ref, wuvt_ref, yc_ref, q_ref, k_ref, vt_ref, *, scale):
    t = pl.program_id(1)
    n_t = pl.num_programs(1)
    rows = x_ref.shape[0]
    g = g_ref[...]
    z = _dot(_rms(x_ref[...], g).astype(BF16), win_ref[...])
    b_gate = z[:, 0:D_CONV]
    u = z[:, D_CONV:2 * D_CONV] * z[:, 2 * D_CONV:3 * D_CONV]
    o_q = 3 * D_CONV
    o_kv = o_q + Q_LORA
    o_kr = o_kv + KV_LORA

    left = jnp.where(t == 0, xm_ref[N_META - HALO:, :], xl_ref[...])
    xh = jnp.concatenate([left, xr_ref[...]], axis=0)
    zh = _dot(_rms(xh, g).astype(BF16), win_ref[:, D_CONV:3 * D_CONV])
    uh = zh[:, :D_CONV] * zh[:, D_CONV:]
    u_before = uh[HALO - 1:HALO, :]
    u_after = jnp.where(t == n_t - 1, 0.0, uh[HALO:HALO + 1, :])
    row = lax.broadcasted_iota(jnp.int32, (rows, 1), 0)
    u_prev = jnp.where(row == 0, u_before, pltpu.roll(u, 1, 0))
    u_next = jnp.where(row == rows - 1, u_after, pltpu.roll(u, rows - 1, 0))
    cw = cw_ref[...]
    yc_ref[...] = (b_gate * (u_prev * cw[0:1, :] + u * cw[1:2, :] + u_next * cw[2:3, :])).astype(BF16)

    cos = cos_ref[...]
    sin = sin_ref[...]
    qq = _dot(_rms(z[:, o_q:o_kv], gq_ref[...]).astype(BF16), wuq_ref[...])
    kvn = _rms(z[:, o_kv:o_kr], gkv_ref[...]).astype(BF16)
    kk = _dot(kvn, wuk_ref[...])
    vt_ref[...] = lax.dot_general(wuvt_ref[...], kvn, _NT, preferred_element_type=F32).astype(BF16)
    k_rope = z[:, o_kr:o_kr + HEAD_PAD] * cos + z[:, o_kr + HEAD_PAD:] * sin
    for h in range(N_HEADS):
        cols = slice(h * HEAD_PAD, (h + 1) * HEAD_PAD)
        rot = slice(D_QK + h * HEAD_PAD, D_QK + (h + 1) * HEAD_PAD)
        q_ref[:, cols] = ((qq[:, cols] * cos + qq[:, rot] * sin) * scale).astype(BF16)
        k_ref[:, cols] = (kk[:, cols] + k_rope).astype(BF16)


def _attention_kernel(q_ref, k_ref, vt_ref, km_ref, vmt_ref, o_ref):
    n_kv, _, kv_block = vt_ref.shape
    items = [(h, c) for h in range(N_HEADS) for c in range(-1, n_kv)]

    def scores(item):
        h, c = item
        cols = slice(h * HEAD_PAD, (h + 1) * HEAD_PAD)
        kc = km_ref[:, cols] if c < 0 else k_ref[c * kv_block:(c + 1) * kv_block, cols]
        s = lax.dot_general(kc, q_ref[:, cols], _NT, preferred_element_type=F32)
        return s, jnp.max(s, axis=0, keepdims=True)

    def accumulate(item, s, s_max, state):
        h, c = item
        vrows = slice(h * V_HEAD, (h + 1) * V_HEAD)
        vtc = vmt_ref[vrows, :] if c < 0 else vt_ref[c, vrows, :]
        m_new = s_max if c < 0 else jnp.maximum(state[0], s_max)
        p = jnp.exp2(s - m_new)
        p_sum = jnp.sum(p, axis=0, keepdims=True)
        pv = _dot(vtc, p.astype(BF16))
        if c < 0:
            return m_new, p_sum, pv
        alpha = jnp.exp2(state[0] - m_new)
        return m_new, alpha * state[1] + p_sum, alpha * state[2] + pv

    ahead = [scores(it) for it in items[:ATTN_LOOKAHEAD]]
    outs, state = [], None
    for i, item in enumerate(items):
        if i + ATTN_LOOKAHEAD < len(items):
            ahead.append(scores(items[i + ATTN_LOOKAHEAD]))
        s, s_max = ahead.pop(0)
        state = accumulate(item, s, s_max, state)
        if item[1] == n_kv - 1:
            outs.append(state[2] / state[1])
    o_ref[...] = jnp.concatenate(outs, axis=0).T.astype(BF16)


def _const_spec(shape):
    return pl.BlockSpec(shape, lambda *_: (0,) * len(shape), pipeline_mode=pl.Buffered(1))


def _params(n_grid):
    return pltpu.CompilerParams(dimension_semantics=("arbitrary",) * n_grid, vmem_limit_bytes=VMEM_LIMIT)


def _ffn1(x, g, wg, wu, wd, rows):
    n = x.shape[0]
    return pl.pallas_call(
        _ffn1_kernel,
        grid=(n // rows,),
        in_specs=[pl.BlockSpec((rows, D_MODEL), lambda i: (i, 0)),
                  _const_spec((1, D_MODEL)), _const_spec((D_MODEL, D_FF)), _const_spec((D_MODEL, D_FF)),
                  _const_spec((D_FF, D_MODEL))],
        out_specs=pl.BlockSpec((rows, D_MODEL), lambda i: (i, 0)),
        out_shape=jax.ShapeDtypeStruct((n, D_MODEL), F32),
        scratch_shapes=[pltpu.VMEM((rows, D_FF), BF16)],
        compiler_params=_params(1),
        name="ffn1",
    )(x, g, wg, wu, wd)


def _out_ffn2(x, yc, ya, wo, g, wg, wu, wd, gf, rows):
    n = x.shape[0]
    return pl.pallas_call(
        _out_ffn2_kernel,
        grid=(n // rows,),
        in_specs=[pl.BlockSpec((rows, D_MODEL), lambda i: (i, 0)),
                  pl.BlockSpec((rows, D_CONV), lambda i: (i, 0)),
                  pl.BlockSpec((rows, D_V), lambda i: (i, 0)),
                  _const_spec((D_CONV + D_V, D_MODEL)),
                  _const_spec((1, D_MODEL)), _const_spec((D_MODEL, D_FF)), _const_spec((D_MODEL, D_FF)),
                  _const_spec((D_FF, D_MODEL)), _const_spec((1, D_MODEL))],
        out_specs=pl.BlockSpec((rows, D_MODEL), lambda i: (i, 0)),
        out_shape=jax.ShapeDtypeStruct((n, D_MODEL), F32),
        scratch_shapes=[pltpu.VMEM((rows, D_FF), BF16)],
        compiler_params=_params(1),
        name="out_ffn2",
    )(x, yc, ya, wo, g, wg, wu, wd, gf)


def _mix_in(x, x_meta, cos, sin, g, win, cw, gq, wuq, gkv, wuk, wuvt, rows, scale):
    bsz, seq, _ = x.shape
    n_t = seq // rows
    per = rows // HALO
    last = seq // HALO - 1
    tile = lambda d: pl.BlockSpec((None, rows, d), lambda b, t: (b, t, 0))
    return pl.pallas_call(
        functools.partial(_mix_in_kernel, scale=scale),
        grid=(bsz, n_t),
        in_specs=[tile(D_MODEL),
                  pl.BlockSpec((None, HALO, D_MODEL), lambda b, t: (b, jnp.maximum(t * per - 1, 0), 0)),
                  pl.BlockSpec((None, HALO, D_MODEL), lambda b, t: (b, jnp.minimum((t + 1) * per, last), 0)),
                  _const_spec((N_META, D_MODEL)),
                  pl.BlockSpec((rows, HEAD_PAD), lambda b, t: (t, 0)),
                  pl.BlockSpec((rows, HEAD_PAD), lambda b, t: (t, 0)),
                  _const_spec((1, D_MODEL)), _const_spec((D_MODEL, D_Z)), _const_spec((3, D_CONV)),
                  _const_spec((1, Q_LORA)), _const_spec((Q_LORA, 2 * D_QK)),
                  _const_spec((1, KV_LORA)), _const_spec((KV_LORA, D_QK)), _const_spec((D_V, KV_LORA))],
        out_specs=[tile(D_CONV), tile(D_QK), tile(D_QK),
                   pl.BlockSpec((None, None, D_V, rows), lambda b, t: (b, t, 0, 0))],
        out_shape=[jax.ShapeDtypeStruct((bsz, seq, D_CONV), BF16),
                   jax.ShapeDtypeStruct((bsz, seq, D_QK), BF16),
                   jax.ShapeDtypeStruct((bsz, seq, D_QK), BF16),
                   jax.ShapeDtypeStruct((bsz, n_t, D_V, rows), BF16)],
        compiler_params=_params(2),
        name="mix_in",
    )(x, x, x, x_meta, cos, sin, g, win, cw, gq, wuq, gkv, wuk, wuvt)


def _attention(q, k, vt, k_meta, vt_meta, q_block):
    bsz, seq, _ = q.shape
    _, n_kv, _, kv_block = vt.shape
    return pl.pallas_call(
        _attention_kernel,
        grid=(bsz, seq // q_block),
        in_specs=[pl.BlockSpec((None, q_block, D_QK), lambda b, t: (b, t, 0)),
                  pl.BlockSpec((None, seq, D_QK), lambda b, t: (b, 0, 0)),
                  pl.BlockSpec((None, n_kv, D_V, kv_block), lambda b, t: (b, 0, 0, 0)),
                  _const_spec((N_META, D_QK)), _const_spec((D_V, N_META))],
        out_specs=pl.BlockSpec((None, q_block, D_V), lambda b, t: (b, t, 0)),
        out_shape=jax.ShapeDtypeStruct((bsz, seq, D_V), BF16),
        compiler_params=_params(2),
        name="attention",
    )(q, k, vt, k_meta, vt_meta)


def _rot_half(w):
    half = QK_ROPE // 2
    return jnp.concatenate([-w[..., half:], w[..., :half]], axis=-1)


def _prepare_weights(w_in, w_uq, w_ukv):
    w_kr = w_in[:, 3 * D_CONV + Q_LORA + KV_LORA:]
    pad_l = jnp.zeros((D_MODEL, QK_NOPE), F32)
    pad_r = jnp.zeros((D_MODEL, HEAD_PAD - QK_NOPE - QK_ROPE), F32)
    win = jnp.concatenate([w_in[:, :3 * D_CONV + Q_LORA + KV_LORA],
                           pad_l, w_kr, pad_r, pad_l, _rot_half(w_kr), pad_r], axis=1)
    wq = w_uq.reshape(Q_LORA, N_HEADS, QK_NOPE + QK_ROPE)
    zq = jnp.zeros((Q_LORA, N_HEADS, HEAD_PAD - QK_NOPE - QK_ROPE), F32)
    q_base = jnp.concatenate([wq, zq], axis=-1).reshape(Q_LORA, D_QK)
    q_rot = jnp.concatenate([jnp.zeros((Q_LORA, N_HEADS, QK_NOPE), F32), _rot_half(wq[..., QK_NOPE:]), zq],
                            axis=-1).reshape(Q_LORA, D_QK)
    wuq = jnp.concatenate([q_base, q_rot], axis=1)
    wkv = w_ukv.reshape(KV_LORA, N_HEADS, QK_NOPE + V_HEAD)
    wuk = jnp.concatenate([wkv[..., :QK_NOPE], jnp.zeros((KV_LORA, N_HEADS, HEAD_PAD - QK_NOPE), F32)],
                          axis=-1).reshape(KV_LORA, D_QK)
    wuvt = wkv[..., QK_NOPE:].reshape(KV_LORA, D_V).T
    return win.astype(BF16), wuq.astype(BF16), wuk.astype(BF16), wuvt.astype(BF16)


def _rope_tables(n_pos):
    pos = jnp.arange(n_pos, dtype=F32)
    inv_freq = 1.0 / (ROPE_BASE ** (jnp.arange(0, QK_ROPE, 2, dtype=F32) / QK_ROPE))
    ang = pos[:, None] * inv_freq[None, :]
    cos, sin = jnp.cos(ang), jnp.sin(ang)
    ones = jnp.ones((n_pos, QK_NOPE), F32)
    zeros = jnp.zeros((n_pos, QK_NOPE), F32)
    pad = jnp.zeros((n_pos, HEAD_PAD - QK_NOPE - QK_ROPE), F32)
    return (jnp.concatenate([ones, cos, cos, pad], axis=1), jnp.concatenate([zeros, sin, sin, pad], axis=1))


def _pick(n, target):
    while n % target:
        target //= 2
    return target


def kernel(x_prompt, x_sample, meta_tokens, ffn1_norm, ffn1_w_gate, ffn1_w_up, ffn1_w_down, mix_norm, w_in, conv_w, q_norm, w_uq, kv_norm, w_ukv, w_out, ffn2_norm, ffn2_w_gate, ffn2_w_up, ffn2_w_down, final_norm):
    assert ffn1_norm.shape[0] == 1, "single layer: meta-token outputs are only consumed as keys/values"
    scale = (QK_NOPE + QK_ROPE) ** -0.5 * LOG2_E
    g1, g2, gm = ffn1_norm, ffn2_norm, mix_norm
    gq, gkv, gf = q_norm, kv_norm, final_norm.reshape(1, D_MODEL)
    wg1, wu1, wd1 = ffn1_w_gate[0].astype(BF16), ffn1_w_up[0].astype(BF16), ffn1_w_down[0].astype(BF16)
    wg2, wu2, wd2 = ffn2_w_gate[0].astype(BF16), ffn2_w_up[0].astype(BF16), ffn2_w_down[0].astype(BF16)
    wo = w_out[0].astype(BF16)
    win, wuq, wuk, wuvt = _prepare_weights(w_in[0], w_uq[0], w_ukv[0])
    cw = conv_w[0]

    max_seq = max(x_prompt.shape[1], x_sample.shape[1])
    cos, sin = _rope_tables(N_META + max_seq)

    xm1 = _ffn1(meta_tokens, g1, wg1, wu1, wd1, N_META)
    _, _, k_meta, vt_meta = _mix_in(xm1[None], xm1, cos[:N_META], sin[:N_META], gm, win, cw, gq, wuq, gkv, wuk, wuvt,
                                    N_META, scale)
    k_meta, vt_meta = k_meta[0], vt_meta[0, 0]

    def trunk(x):
        bsz, seq, _ = x.shape
        rows = _pick(seq, 512)
        x1 = _ffn1(x.reshape(bsz * seq, D_MODEL), g1, wg1, wu1, wd1, rows)
        yc, q, k, vt = _mix_in(x1.reshape(bsz, seq, D_MODEL), xm1, cos[N_META:N_META + seq], sin[N_META:N_META + seq],
                               gm, win, cw, gq, wuq, gkv, wuk, wuvt, rows, scale)
        ya = _attention(q, k, vt, k_meta, vt_meta, _pick(seq, 512))
        y = _out_ffn2(x1, yc.reshape(bsz * seq, D_CONV), ya.reshape(bsz * seq, D_V), wo, g2, wg2, wu2, wd2, gf, rows)
        return y.reshape(bsz, seq, D_MODEL)

    return (trunk(x_prompt), trunk(x_sample))
```

```python
import functools

import jax
import jax.numpy as jnp
from jax import lax
from jax.experimental import pallas as pl
from jax.experimental.pallas import tpu as pltpu

D_MODEL = 1024
N_META = 16
D_CONV = 512
N_HEADS = 8
QK_NOPE = 64
QK_ROPE = 32
V_HEAD = 64
Q_LORA = 384
KV_LORA = 256
D_FF = 2816
ROPE_BASE = 10000.0
EPS = 1e-6
LOG2_E = 1.4426950408889634

HEAD_PAD = 128
D_QK = N_HEADS * HEAD_PAD
D_V = N_HEADS * V_HEAD
D_Z = 3 * D_CONV + Q_LORA + KV_LORA + 2 * HEAD_PAD
FF_CHUNK = 256
HALO = 8
BF16_ROWS = 16
VMEM_LIMIT = 56 * 1024 * 1024
ATTN_KV_BLOCK = 256
ATTN_LOOKAHEAD = 2

BF16 = jnp.bfloat16
F32 = jnp.float32
_NT = (((1,), (1,)), ((), ()))


def _dot(a, b):
    return jnp.dot(a, b, preferred_element_type=F32)


def _rms(x, g):
    ms = jnp.mean(x * x, axis=-1, keepdims=True)
    return x * lax.rsqrt(ms + EPS) * g


def _swiglu_half_step(x, g_ref, wg_ref, wu_ref, wd_ref, h_ref):
    xn = _rms(x, g_ref[...]).astype(BF16)
    for c in range(D_FF // FF_CHUNK):
        cols = slice(c * FF_CHUNK, (c + 1) * FF_CHUNK)
        gate = _dot(xn, wg_ref[:, cols])
        up = _dot(xn, wu_ref[:, cols])
        h_ref[:, cols] = (gate * jax.nn.sigmoid(gate) * up).astype(BF16)
    return x + 0.5 * _dot(h_ref[...], wd_ref[...])


def _ffn1_kernel(x_ref, g_ref, wg_ref, wu_ref, wd_ref, o_ref, h_ref):
    o_ref[...] = _swiglu_half_step(x_ref[...], g_ref, wg_ref, wu_ref, wd_ref, h_ref)


def _out_ffn2_kernel(x_ref, yc_ref, ya_ref, wo_ref, g_ref, wg_ref, wu_ref, wd_ref, gf_ref, o_ref, h_ref):
    x = x_ref[...] + _dot(yc_ref[...], wo_ref[:D_CONV, :]) + _dot(ya_ref[...], wo_ref[D_CONV:, :])
    x = _swiglu_half_step(x, g_ref, wg_ref, wu_ref, wd_ref, h_ref)
    o_ref[...] = _rms(x, gf_ref[...])


def _mix_in_kernel(x_ref, xl_ref, xr_ref, xm_ref, cos_ref, sin_ref, g_ref, win_ref, cw_ref,
                   gq_ref, wuq_ref, gkv_ref, wuk_ref, wuvt_ref, yc_ref, q_ref, k_ref, vt_ref, *, scale):
    t = pl.program_id(1)
    n_t = pl.num_programs(1)
    rows = x_ref.shape[0]
    g = g_ref[...]
    z = _dot(_rms(x_ref[...], g).astype(BF16), win_ref[...])
    b_gate = z[:, 0:D_CONV]
    u = z[:, D_CONV:2 * D_CONV] * z[:, 2 * D_CONV:3 * D_CONV]
    o_q = 3 * D_CONV
    o_kv = o_q + Q_LORA
    o_kr = o_kv + KV_LORA

    left = jnp.where(t == 0, xm_ref[N_META - HALO:, :], xl_ref[...])
    xh = jnp.concatenate([left, xr_ref[...]], axis=0)
    zh = _dot(_rms(xh, g).astype(BF16), win_ref[:, D_CONV:3 * D_CONV])
    uh = zh[:, :D_CONV] * zh[:, D_CONV:]
    u_before = uh[HALO - 1:HALO, :]
    u_after = jnp.where(t == n_t - 1, 0.0, uh[HALO:HALO + 1, :])
    row = lax.broadcasted_iota(jnp.int32, (rows, 1), 0)
    u_prev = jnp.where(row == 0, u_before, pltpu.roll(u, 1, 0))
    u_next = jnp.where(row == rows - 1, u_after, pltpu.roll(u, rows - 1, 0))
    cw = cw_ref[...]
    yc_ref[...] = (b_gate * (u_prev * cw[0:1, :] + u * cw[1:2, :] + u_next * cw[2:3, :])).astype(BF16)

    cos = cos_ref[...]
    sin = sin_ref[...]
    qq = _dot(_rms(z[:, o_q:o_kv], gq_ref[...]).astype(BF16), wuq_ref[...])
    kvn = _rms(z[:, o_kv:o_kr], gkv_ref[...]).astype(BF16)
    kk = _dot(kvn, wuk_ref[...])
    vt_ref[...] = lax.dot_general(wuvt_ref[...], kvn, _NT, preferred_element_type=F32).astype(BF16)
    k_rope = z[:, o_kr:o_kr + HEAD_PAD] * cos + z[:, o_kr + HEAD_PAD:] * sin
    for h in range(N_HEADS):
        cols = slice(h * HEAD_PAD, (h + 1) * HEAD_PAD)
        rot = slice(D_QK + h * HEAD_PAD, D_QK + (h + 1) * HEAD_PAD)
        q_ref[:, cols] = ((qq[:, cols] * cos + qq[:, rot] * sin) * scale).astype(BF16)
        k_ref[:, cols] = (kk[:, cols] + k_rope).astype(BF16)


def _attention_kernel(q_ref, k_ref, vt_ref, km_ref, vmt_ref, o_ref):
    n_kv, _, kv_chunk = vt_ref.shape
    sub = min(ATTN_KV_BLOCK, kv_chunk)
    items = [(h, c, j) for h in range(N_HEADS) for c in range(-1, n_kv)
             for j in range(1 if c < 0 else kv_chunk // sub)]
    last = (n_kv - 1, kv_chunk // sub - 1)

    def scores(item):
        h, c, j = item
        cols = slice(h * HEAD_PAD, (h + 1) * HEAD_PAD)
        kc = km_ref[:, cols] if c < 0 else k_ref[c * kv_chunk + j * sub:c * kv_chunk + (j + 1) * sub, cols]
        s = lax.dot_general(kc, q_ref[:, cols], _NT, preferred_element_type=F32)
        return s, jnp.max(s, axis=0, keepdims=True)

    def accumulate(item, s, s_max, state):
        h, c, j = item
        vrows = slice(h * V_HEAD, (h + 1) * V_HEAD)
        vtc = vmt_ref[vrows, :] if c < 0 else vt_ref[c, vrows, j * sub:(j + 1) * sub]
        vtc = jnp.concatenate([vtc, jnp.ones((BF16_ROWS, vtc.shape[1]), BF16)], axis=0)
        m_new = s_max if c < 0 else jnp.maximum(state[0], s_max)
        p = jnp.exp2(s - m_new)
        pv = _dot(vtc, p.astype(BF16))
        if c < 0:
            return m_new, pv
        return m_new, jnp.exp2(state[0] - m_new) * state[1] + pv

    ahead = [scores(it) for it in items[:ATTN_LOOKAHEAD]]
    outs, state = [], None
    for i, item in enumerate(items):
        if i + ATTN_LOOKAHEAD < len(items):
            ahead.append(scores(items[i + ATTN_LOOKAHEAD]))
        s, s_max = ahead.pop(0)
        state = accumulate(item, s, s_max, state)
        if item[1:] == last:
            outs.append(state[1][:V_HEAD] / state[1][V_HEAD:V_HEAD + 1])
    o_ref[...] = jnp.concatenate(outs, axis=0).T.astype(BF16)


def _const_spec(shape):
    return pl.BlockSpec(shape, lambda *_: (0,) * len(shape), pipeline_mode=pl.Buffered(1))


def _params(n_grid):
    return pltpu.CompilerParams(dimension_semantics=("arbitrary",) * n_grid, vmem_limit_bytes=VMEM_LIMIT)


def _ffn1(x, g, wg, wu, wd, rows):
    n = x.shape[0]
    return pl.pallas_call(
        _ffn1_kernel,
        grid=(n // rows,),
        in_specs=[pl.BlockSpec((rows, D_MODEL), lambda i: (i, 0)),
                  _const_spec((1, D_MODEL)), _const_spec((D_MODEL, D_FF)), _const_spec((D_MODEL, D_FF)),
                  _const_spec((D_FF, D_MODEL))],
        out_specs=pl.BlockSpec((rows, D_MODEL), lambda i: (i, 0)),
        out_shape=jax.ShapeDtypeStruct((n, D_MODEL), F32),
        scratch_shapes=[pltpu.VMEM((rows, D_FF), BF16)],
        compiler_params=_params(1),
        name="ffn1",
    )(x, g, wg, wu, wd)


def _out_ffn2(x, yc, ya, wo, g, wg, wu, wd, gf, rows):
    n = x.shape[0]
    return pl.pallas_call(
        _out_ffn2_kernel,
        grid=(n // rows,),
        in_specs=[pl.BlockSpec((rows, D_MODEL), lambda i: (i, 0)),
                  pl.BlockSpec((rows, D_CONV), lambda i: (i, 0)),
                  pl.BlockSpec((rows, D_V), lambda i: (i, 0)),
                  _const_spec((D_CONV + D_V, D_MODEL)),
                  _const_spec((1, D_MODEL)), _const_spec((D_MODEL, D_FF)), _const_spec((D_MODEL, D_FF)),
                  _const_spec((D_FF, D_MODEL)), _const_spec((1, D_MODEL))],
        out_specs=pl.BlockSpec((rows, D_MODEL), lambda i: (i, 0)),
        out_shape=jax.ShapeDtypeStruct((n, D_MODEL), F32),
        scratch_shapes=[pltpu.VMEM((rows, D_FF), BF16)],
        compiler_params=_params(1),
        name="out_ffn2",
    )(x, yc, ya, wo, g, wg, wu, wd, gf)


def _mix_in(x, x_meta, cos, sin, g, win, cw, gq, wuq, gkv, wuk, wuvt, rows, scale):
    bsz, seq, _ = x.shape
    n_t = seq // rows
    per = rows // HALO
    last = seq // HALO - 1
    tile = lambda d: pl.BlockSpec((None, rows, d), lambda b, t: (b, t, 0))
    return pl.pallas_call(
        functools.partial(_mix_in_kernel, scale=scale),
        grid=(bsz, n_t),
        in_specs=[tile(D_MODEL),
                  pl.BlockSpec((None, HALO, D_MODEL), lambda b, t: (b, jnp.maximum(t * per - 1, 0), 0)),
                  pl.BlockSpec((None, HALO, D_MODEL), lambda b, t: (b, jnp.minimum((t + 1) * per, last), 0)),
                  _const_spec((N_META, D_MODEL)),
                  pl.BlockSpec((rows, HEAD_PAD), lambda b, t: (t, 0)),
                  pl.BlockSpec((rows, HEAD_PAD), lambda b, t: (t, 0)),
                  _const_spec((1, D_MODEL)), _const_spec((D_MODEL, D_Z)), _const_spec((3, D_CONV)),
                  _const_spec((1, Q_LORA)), _const_spec((Q_LORA, 2 * D_QK)),
                  _const_spec((1, KV_LORA)), _const_spec((KV_LORA, D_QK)), _const_spec((D_V, KV_LORA))],
        out_specs=[tile(D_CONV), tile(D_QK), tile(D_QK),
                   pl.BlockSpec((None, None, D_V, rows), lambda b, t: (b, t, 0, 0))],
        out_shape=[jax.ShapeDtypeStruct((bsz, seq, D_CONV), BF16),
                   jax.ShapeDtypeStruct((bsz, seq, D_QK), BF16),
                   jax.ShapeDtypeStruct((bsz, seq, D_QK), BF16),
                   jax.ShapeDtypeStruct((bsz, n_t, D_V, rows), BF16)],
        compiler_params=_params(2),
        name="mix_in",
    )(x, x, x, x_meta, cos, sin, g, win, cw, gq, wuq, gkv, wuk, wuvt)


def _attention(q, k, vt, k_meta, vt_meta, q_block):
    bsz, seq, _ = q.shape
    _, n_kv, _, kv_block = vt.shape
    return pl.pallas_call(
        _attention_kernel,
        grid=(bsz, seq // q_block),
        in_specs=[pl.BlockSpec((None, q_block, D_QK), lambda b, t: (b, t, 0)),
                  pl.BlockSpec((None, seq, D_QK), lambda b, t: (b, 0, 0)),
                  pl.BlockSpec((None, n_kv, D_V, kv_block), lambda b, t: (b, 0, 0, 0)),
                  _const_spec((N_META, D_QK)), _const_spec((D_V, N_META))],
        out_specs=pl.BlockSpec((None, q_block, D_V), lambda b, t: (b, t, 0)),
        out_shape=jax.ShapeDtypeStruct((bsz, seq, D_V), BF16),
        compiler_params=_params(2),
        name="attention",
    )(q, k, vt, k_meta, vt_meta)


def _rot_half(w):
    half = QK_ROPE // 2
    return jnp.concatenate([-w[..., half:], w[..., :half]], axis=-1)


def _prepare_weights(w_in, w_uq, w_ukv):
    w_kr = w_in[:, 3 * D_CONV + Q_LORA + KV_LORA:]
    pad_l = jnp.zeros((D_MODEL, QK_NOPE), F32)
    pad_r = jnp.zeros((D_MODEL, HEAD_PAD - QK_NOPE - QK_ROPE), F32)
    win = jnp.concatenate([w_in[:, :3 * D_CONV + Q_LORA + KV_LORA],
                           pad_l, w_kr, pad_r, pad_l, _rot_half(w_kr), pad_r], axis=1)
    wq = w_uq.reshape(Q_LORA, N_HEADS, QK_NOPE + QK_ROPE)
    zq = jnp.zeros((Q_LORA, N_HEADS, HEAD_PAD - QK_NOPE - QK_ROPE), F32)
    q_base = jnp.concatenate([wq, zq], axis=-1).reshape(Q_LORA, D_QK)
    q_rot = jnp.concatenate([jnp.zeros((Q_LORA, N_HEADS, QK_NOPE), F32), _rot_half(wq[..., QK_NOPE:]), zq],
                            axis=-1).reshape(Q_LORA, D_QK)
    wuq = jnp.concatenate([q_base, q_rot], axis=1)
    wkv = w_ukv.reshape(KV_LORA, N_HEADS, QK_NOPE + V_HEAD)
    wuk = jnp.concatenate([wkv[..., :QK_NOPE], jnp.zeros((KV_LORA, N_HEADS, HEAD_PAD - QK_NOPE), F32)],
                          axis=-1).reshape(KV_LORA, D_QK)
    wuvt = wkv[..., QK_NOPE:].reshape(KV_LORA, D_V).T
    return win.astype(BF16), wuq.astype(BF16), wuk.astype(BF16), wuvt.astype(BF16)


def _rope_tables(n_pos):
    pos = jnp.arange(n_pos, dtype=F32)
    inv_freq = 1.0 / (ROPE_BASE ** (jnp.arange(0, QK_ROPE, 2, dtype=F32) / QK_ROPE))
    ang = pos[:, None] * inv_freq[None, :]
    cos, sin = jnp.cos(ang), jnp.sin(ang)
    ones = jnp.ones((n_pos, QK_NOPE), F32)
    zeros = jnp.zeros((n_pos, QK_NOPE), F32)
    pad = jnp.zeros((n_pos, HEAD_PAD - QK_NOPE - QK_ROPE), F32)
    return (jnp.concatenate([ones, cos, cos, pad], axis=1), jnp.concatenate([zeros, sin, sin, pad], axis=1))


def _pick(n, target):
    while n % target:
        target //= 2
    return target


def kernel(x_prompt, x_sample, meta_tokens, ffn1_norm, ffn1_w_gate, ffn1_w_up, ffn1_w_down, mix_norm, w_in, conv_w, q_norm, w_uq, kv_norm, w_ukv, w_out, ffn2_norm, ffn2_w_gate, ffn2_w_up, ffn2_w_down, final_norm):
    assert ffn1_norm.shape[0] == 1, "single layer: meta-token outputs are only consumed as keys/values"
    scale = (QK_NOPE + QK_ROPE) ** -0.5 * LOG2_E
    g1, g2, gm = ffn1_norm, ffn2_norm, mix_norm
    gq, gkv, gf = q_norm, kv_norm, final_norm.reshape(1, D_MODEL)
    wg1, wu1, wd1 = ffn1_w_gate[0].astype(BF16), ffn1_w_up[0].astype(BF16), ffn1_w_down[0].astype(BF16)
    wg2, wu2, wd2 = ffn2_w_gate[0].astype(BF16), ffn2_w_up[0].astype(BF16), ffn2_w_down[0].astype(BF16)
    wo = w_out[0].astype(BF16)
    win, wuq, wuk, wuvt = _prepare_weights(w_in[0], w_uq[0], w_ukv[0])
    cw = conv_w[0]

    max_seq = max(x_prompt.shape[1], x_sample.shape[1])
    cos, sin = _rope_tables(N_META + max_seq)

    xm1 = _ffn1(meta_tokens, g1, wg1, wu1, wd1, N_META)
    _, _, k_meta, vt_meta = _mix_in(xm1[None], xm1, cos[:N_META], sin[:N_META], gm, win, cw, gq, wuq, gkv, wuk, wuvt,
                                    N_META, scale)
    k_meta, vt_meta = k_meta[0], vt_meta[0, 0]

    def trunk(x):
        bsz, seq, _ = x.shape
        rows = _pick(seq, 512)
        x1 = _ffn1(x.reshape(bsz * seq, D_MODEL), g1, wg1, wu1, wd1, rows)
        yc, q, k, vt = _mix_in(x1.reshape(bsz, seq, D_MODEL), xm1, cos[N_META:N_META + seq], sin[N_META:N_META + seq],
                               gm, win, cw, gq, wuq, gkv, wuk, wuvt, rows, scale)
        ya = _attention(q, k, vt, k_meta, vt_meta, _pick(seq, 512))
        y = _out_ffn2(x1, yc.reshape(bsz * seq, D_CONV), ya.reshape(bsz * seq, D_V), wo, g2, wg2, wu2, wd2, gf, rows)
        return y.reshape(bsz, seq, D_MODEL)

    return (trunk(x_prompt), trunk(x_sample))
```

```python
import functools

import jax
import jax.numpy as jnp
from jax import lax
from jax.experimental import pallas as pl
from jax.experimental.pallas import tpu as pltpu

D_MODEL = 1024
N_META = 16
D_CONV = 512
N_HEADS = 8
QK_NOPE = 64
QK_ROPE = 32
V_HEAD = 64
Q_LORA = 384
KV_LORA = 256
D_FF = 2816
ROPE_BASE = 10000.0
EPS = 1e-6
LOG2_E = 1.4426950408889634

HEAD_PAD = 128
D_QK = N_HEADS * HEAD_PAD
D_V = N_HEADS * V_HEAD
D_Z = 3 * D_CONV + Q_LORA + KV_LORA + 2 * HEAD_PAD
FF_CHUNK = 256
HALO = 8
BF16_ROWS = 16
MIX_ROW_PARTS = 2
VMEM_LIMIT = 56 * 1024 * 1024
ATTN_KV_BLOCK = 256
ATTN_LOOKAHEAD = 2

BF16 = jnp.bfloat16
F32 = jnp.float32
_NT = (((1,), (1,)), ((), ()))


def _dot(a, b):
    return jnp.dot(a, b, preferred_element_type=F32)


def _rms(x, g):
    ms = jnp.mean(x * x, axis=-1, keepdims=True)
    return x * lax.rsqrt(ms + EPS) * g


def _swiglu_half_step(x, g_ref, wg_ref, wu_ref, wd_ref, h_ref):
    xn = _rms(x, g_ref[...]).astype(BF16)
    for c in range(D_FF // FF_CHUNK):
        cols = slice(c * FF_CHUNK, (c + 1) * FF_CHUNK)
        gate = _dot(xn, wg_ref[:, cols])
        up = _dot(xn, wu_ref[:, cols])
        h_ref[:, cols] = (gate * jax.nn.sigmoid(gate) * up).astype(BF16)
    return x + 0.5 * _dot(h_ref[...], wd_ref[...])


def _ffn1_kernel(x_ref, g_ref, wg_ref, wu_ref, wd_ref, o_ref, h_ref):
    o_ref[...] = _swiglu_half_step(x_ref[...], g_ref, wg_ref, wu_ref, wd_ref, h_ref)


def _out_ffn2_kernel(x_ref, yc_ref, ya_ref, wo_ref, g_ref, wg_ref, wu_ref, wd_ref, gf_ref, o_ref, h_ref):
    x = x_ref[...] + _dot(yc_ref[...], wo_ref[:D_CONV, :]) + _dot(ya_ref[...], wo_ref[D_CONV:, :])
    x = _swiglu_half_step(x, g_ref, wg_ref, wu_ref, wd_ref, h_ref)
    o_ref[...] = _rms(x, gf_ref[...])


def _mix_in_kernel(x_ref, xl_ref, xr_ref, xm_ref, cos_ref, sin_ref, cost_ref, sint_ref, g_ref, win_ref, cw_ref,
                   gq_ref, wuqt_ref, gkv_ref, wuk_ref, wuvt_ref, yc_ref, qt_ref, k_ref, vt_ref, *, scale):
    t = pl.program_id(1)
    n_t = pl.num_programs(1)
    rows = x_ref.shape[0]
    g = g_ref[...]
    n_parts = MIX_ROW_PARTS if rows % (MIX_ROW_PARTS * HEAD_PAD) == 0 else 1
    parts = [slice(r, r + rows // n_parts) for r in range(0, rows, rows // n_parts)]
    z = jnp.concatenate([_dot(_rms(x_ref[r, :], g).astype(BF16), win_ref[...]) for r in parts], axis=0)
    b_gate = z[:, 0:D_CONV]
    u = z[:, D_CONV:2 * D_CONV] * z[:, 2 * D_CONV:3 * D_CONV]
    o_q = 3 * D_CONV
    o_kv = o_q + Q_LORA
    o_kr = o_kv + KV_LORA

    left = jnp.where(t == 0, xm_ref[N_META - HALO:, :], xl_ref[...])
    xh = jnp.concatenate([left, xr_ref[...]], axis=0)
    zh = _dot(_rms(xh, g).astype(BF16), win_ref[:, D_CONV:3 * D_CONV])
    uh = zh[:, :D_CONV] * zh[:, D_CONV:]
    u_before = uh[HALO - 1:HALO, :]
    u_after = jnp.where(t == n_t - 1, 0.0, uh[HALO:HALO + 1, :])
    row = lax.broadcasted_iota(jnp.int32, (rows, 1), 0)
    u_prev = jnp.where(row == 0, u_before, pltpu.roll(u, 1, 0))
    u_next = jnp.where(row == rows - 1, u_after, pltpu.roll(u, rows - 1, 0))
    cw = cw_ref[...]
    yc_ref[...] = (b_gate * (u_prev * cw[0:1, :] + u * cw[1:2, :] + u_next * cw[2:3, :])).astype(BF16)

    for r in parts:
        cos = cos_ref[r, :]
        sin = sin_ref[r, :]
        cost = cost_ref[:, r]
        sint = sint_ref[:, r]
        qn = _rms(z[r, o_q:o_kv], gq_ref[...]).astype(BF16)
        qqt = lax.dot_general(wuqt_ref[...], qn, _NT, preferred_element_type=F32)
        kvn = _rms(z[r, o_kv:o_kr], gkv_ref[...]).astype(BF16)
        kk = _dot(kvn, wuk_ref[...])
        vt_ref[:, r] = lax.dot_general(wuvt_ref[...], kvn, _NT, preferred_element_type=F32).astype(BF16)
        k_rope = z[r, o_kr:o_kr + HEAD_PAD] * cos + z[r, o_kr + HEAD_PAD:] * sin
        for h in range(N_HEADS):
            cols = slice(h * HEAD_PAD, (h + 1) * HEAD_PAD)
            rot = slice(D_QK + h * HEAD_PAD, D_QK + (h + 1) * HEAD_PAD)
            qt_ref[cols, r] = ((qqt[cols, :] * cost + qqt[rot, :] * sint) * scale).astype(BF16)
            k_ref[r, cols] = (kk[:, cols] + k_rope).astype(BF16)


def _attention_kernel(qt_ref, k_ref, vt_ref, km_ref, vmt_ref, o_ref):
    n_kv, _, kv_chunk = vt_ref.shape
    sub = min(ATTN_KV_BLOCK, kv_chunk)
    items = [(h, c, j) for h in range(N_HEADS) for c in range(-1, n_kv)
             for j in range(1 if c < 0 else kv_chunk // sub)]
    last = (n_kv - 1, kv_chunk // sub - 1)

    def scores(item):
        h, c, j = item
        cols = slice(h * HEAD_PAD, (h + 1) * HEAD_PAD)
        kc = km_ref[:, cols] if c < 0 else k_ref[c * kv_chunk + j * sub:c * kv_chunk + (j + 1) * sub, cols]
        s = _dot(kc, qt_ref[cols, :])
        return s, jnp.max(s, axis=0, keepdims=True)

    def weighted_values(item, p):
        h, c, j = item
        vrows = slice(h * V_HEAD, (h + 1) * V_HEAD)
        vtc = vmt_ref[vrows, :] if c < 0 else vt_ref[c, vrows, j * sub:(j + 1) * sub]
        vtc = jnp.concatenate([vtc, jnp.ones((BF16_ROWS, vtc.shape[1]), BF16)], axis=0)
        return _dot(vtc, p)

    ahead = [scores(it) for it in items[:ATTN_LOOKAHEAD]]
    outs, m, acc = [], None, None

    def drain(done, acc):
        item, p, alpha = done
        pv = weighted_values(item, p)
        acc = pv if alpha is None else alpha * acc + pv
        if item[1:] == last:
            outs.append(acc[:V_HEAD] / acc[V_HEAD:V_HEAD + 1])
        return acc

    for i, item in enumerate(items):
        if i + ATTN_LOOKAHEAD < len(items):
            ahead.append(scores(items[i + ATTN_LOOKAHEAD]))
        s, s_max = ahead.pop(0)
        first = item[1] < 0
        m_new = s_max if first else jnp.maximum(m, s_max)
        alpha = None if first else jnp.exp2(m - m_new)
        m = m_new
        p = jnp.exp2(s - m_new).astype(BF16)
        acc = drain((item, p, alpha), acc)
    o_ref[...] = jnp.concatenate(outs, axis=0).T.astype(BF16)


def _const_spec(shape):
    return pl.BlockSpec(shape, lambda *_: (0,) * len(shape), pipeline_mode=pl.Buffered(1))


def _params(n_grid):
    return pltpu.CompilerParams(dimension_semantics=("arbitrary",) * n_grid, vmem_limit_bytes=VMEM_LIMIT)


def _ffn1(x, g, wg, wu, wd, rows):
    n = x.shape[0]
    return pl.pallas_call(
        _ffn1_kernel,
        grid=(n // rows,),
        in_specs=[pl.BlockSpec((rows, D_MODEL), lambda i: (i, 0)),
                  _const_spec((1, D_MODEL)), _const_spec((D_MODEL, D_FF)), _const_spec((D_MODEL, D_FF)),
                  _const_spec((D_FF, D_MODEL))],
        out_specs=pl.BlockSpec((rows, D_MODEL), lambda i: (i, 0)),
        out_shape=jax.ShapeDtypeStruct((n, D_MODEL), F32),
        scratch_shapes=[pltpu.VMEM((rows, D_FF), BF16)],
        compiler_params=_params(1),
        name="ffn1",
    )(x, g, wg, wu, wd)


def _out_ffn2(x, yc, ya, wo, g, wg, wu, wd, gf, rows):
    n = x.shape[0]
    return pl.pallas_call(
        _out_ffn2_kernel,
        grid=(n // rows,),
        in_specs=[pl.BlockSpec((rows, D_MODEL), lambda i: (i, 0)),
                  pl.BlockSpec((rows, D_CONV), lambda i: (i, 0)),
                  pl.BlockSpec((rows, D_V), lambda i: (i, 0)),
                  _const_spec((D_CONV + D_V, D_MODEL)),
                  _const_spec((1, D_MODEL)), _const_spec((D_MODEL, D_FF)), _const_spec((D_MODEL, D_FF)),
                  _const_spec((D_FF, D_MODEL)), _const_spec((1, D_MODEL))],
        out_specs=pl.BlockSpec((rows, D_MODEL), lambda i: (i, 0)),
        out_shape=jax.ShapeDtypeStruct((n, D_MODEL), F32),
        scratch_shapes=[pltpu.VMEM((rows, D_FF), BF16)],
        compiler_params=_params(1),
        name="out_ffn2",
    )(x, yc, ya, wo, g, wg, wu, wd, gf)


def _mix_in(x, x_meta, cos, sin, g, win, cw, gq, wuqt, gkv, wuk, wuvt, rows, scale):
    bsz, seq, _ = x.shape
    n_t = seq // rows
    per = rows // HALO
    last = seq // HALO - 1
    tile = lambda d: pl.BlockSpec((None, rows, d), lambda b, t: (b, t, 0))
    return pl.pallas_call(
        functools.partial(_mix_in_kernel, scale=scale),
        grid=(bsz, n_t),
        in_specs=[tile(D_MODEL),
                  pl.BlockSpec((None, HALO, D_MODEL), lambda b, t: (b, jnp.maximum(t * per - 1, 0), 0)),
                  pl.BlockSpec((None, HALO, D_MODEL), lambda b, t: (b, jnp.minimum((t + 1) * per, last), 0)),
                  _const_spec((N_META, D_MODEL)),
                  pl.BlockSpec((rows, HEAD_PAD), lambda b, t: (t, 0)),
                  pl.BlockSpec((rows, HEAD_PAD), lambda b, t: (t, 0)),
                  pl.BlockSpec((HEAD_PAD, rows), lambda b, t: (0, t)),
                  pl.BlockSpec((HEAD_PAD, rows), lambda b, t: (0, t)),
                  _const_spec((1, D_MODEL)), _const_spec((D_MODEL, D_Z)), _const_spec((3, D_CONV)),
                  _const_spec((1, Q_LORA)), _const_spec((2 * D_QK, Q_LORA)),
                  _const_spec((1, KV_LORA)), _const_spec((KV_LORA, D_QK)), _const_spec((D_V, KV_LORA))],
        out_specs=[tile(D_CONV), pl.BlockSpec((None, D_QK, rows), lambda b, t: (b, 0, t)), tile(D_QK),
                   pl.BlockSpec((None, None, D_V, rows), lambda b, t: (b, t, 0, 0))],
        out_shape=[jax.ShapeDtypeStruct((bsz, seq, D_CONV), BF16),
                   jax.ShapeDtypeStruct((bsz, D_QK, seq), BF16),
                   jax.ShapeDtypeStruct((bsz, seq, D_QK), BF16),
                   jax.ShapeDtypeStruct((bsz, n_t, D_V, rows), BF16)],
        compiler_params=_params(2),
        name="mix_in",
    )(x, x, x, x_meta, cos, sin, cos.T, sin.T, g, win, cw, gq, wuqt, gkv, wuk, wuvt)


def _attention(qt, k, vt, k_meta, vt_meta, q_block):
    bsz, _, seq = qt.shape
    _, n_kv, _, kv_block = vt.shape
    return pl.pallas_call(
        _attention_kernel,
        grid=(bsz, seq // q_block),
        in_specs=[pl.BlockSpec((None, D_QK, q_block), lambda b, t: (b, 0, t)),
                  pl.BlockSpec((None, seq, D_QK), lambda b, t: (b, 0, 0)),
                  pl.BlockSpec((None, n_kv, D_V, kv_block), lambda b, t: (b, 0, 0, 0)),
                  _const_spec((N_META, D_QK)), _const_spec((D_V, N_META))],
        out_specs=pl.BlockSpec((None, q_block, D_V), lambda b, t: (b, t, 0)),
        out_shape=jax.ShapeDtypeStruct((bsz, seq, D_V), BF16),
        compiler_params=_params(2),
        name="attention",
    )(qt, k, vt, k_meta, vt_meta)


def _rot_half(w):
    half = QK_ROPE // 2
    return jnp.concatenate([-w[..., half:], w[..., :half]], axis=-1)


def _prepare_weights(w_in, w_uq, w_ukv):
    w_kr = w_in[:, 3 * D_CONV + Q_LORA + KV_LORA:]
    pad_l = jnp.zeros((D_MODEL, QK_NOPE), F32)
    pad_r = jnp.zeros((D_MODEL, HEAD_PAD - QK_NOPE - QK_ROPE), F32)
    win = jnp.concatenate([w_in[:, :3 * D_CONV + Q_LORA + KV_LORA],
                           pad_l, w_kr, pad_r, pad_l, _rot_half(w_kr), pad_r], axis=1)
    wq = w_uq.reshape(Q_LORA, N_HEADS, QK_NOPE + QK_ROPE)
    zq = jnp.zeros((Q_LORA, N_HEADS, HEAD_PAD - QK_NOPE - QK_ROPE), F32)
    q_base = jnp.concatenate([wq, zq], axis=-1).reshape(Q_LORA, D_QK)
    q_rot = jnp.concatenate([jnp.zeros((Q_LORA, N_HEADS, QK_NOPE), F32), _rot_half(wq[..., QK_NOPE:]), zq],
                            axis=-1).reshape(Q_LORA, D_QK)
    wuqt = jnp.concatenate([q_base, q_rot], axis=1).T
    wkv = w_ukv.reshape(KV_LORA, N_HEADS, QK_NOPE + V_HEAD)
    wuk = jnp.concatenate([wkv[..., :QK_NOPE], jnp.zeros((KV_LORA, N_HEADS, HEAD_PAD - QK_NOPE), F32)],
                          axis=-1).reshape(KV_LORA, D_QK)
    wuvt = wkv[..., QK_NOPE:].reshape(KV_LORA, D_V).T
    return win.astype(BF16), wuqt.astype(BF16), wuk.astype(BF16), wuvt.astype(BF16)


def _rope_tables(n_pos):
    pos = jnp.arange(n_pos, dtype=F32)
    inv_freq = 1.0 / (ROPE_BASE ** (jnp.arange(0, QK_ROPE, 2, dtype=F32) / QK_ROPE))
    ang = pos[:, None] * inv_freq[None, :]
    cos, sin = jnp.cos(ang), jnp.sin(ang)
    ones = jnp.ones((n_pos, QK_NOPE), F32)
    zeros = jnp.zeros((n_pos, QK_NOPE), F32)
    pad = jnp.zeros((n_pos, HEAD_PAD - QK_NOPE - QK_ROPE), F32)
    return (jnp.concatenate([ones, cos, cos, pad], axis=1), jnp.concatenate([zeros, sin, sin, pad], axis=1))


def _pick(n, target):
    while n % target:
        target //= 2
    return target


def kernel(x_prompt, x_sample, meta_tokens, ffn1_norm, ffn1_w_gate, ffn1_w_up, ffn1_w_down, mix_norm, w_in, conv_w, q_norm, w_uq, kv_norm, w_ukv, w_out, ffn2_norm, ffn2_w_gate, ffn2_w_up, ffn2_w_down, final_norm):
    assert ffn1_norm.shape[0] == 1, "single layer: meta-token outputs are only consumed as keys/values"
    scale = (QK_NOPE + QK_ROPE) ** -0.5 * LOG2_E
    g1, g2, gm = ffn1_norm, ffn2_norm, mix_norm
    gq, gkv, gf = q_norm, kv_norm, final_norm.reshape(1, D_MODEL)
    wg1, wu1, wd1 = ffn1_w_gate[0].astype(BF16), ffn1_w_up[0].astype(BF16), ffn1_w_down[0].astype(BF16)
    wg2, wu2, wd2 = ffn2_w_gate[0].astype(BF16), ffn2_w_up[0].astype(BF16), ffn2_w_down[0].astype(BF16)
    wo = w_out[0].astype(BF16)
    win, wuqt, wuk, wuvt = _prepare_weights(w_in[0], w_uq[0], w_ukv[0])
    cw = conv_w[0]

    max_seq = max(x_prompt.shape[1], x_sample.shape[1])
    cos, sin = _rope_tables(N_META + max_seq)

    xm1 = _ffn1(meta_tokens, g1, wg1, wu1, wd1, N_META)
    _, _, k_meta, vt_meta = _mix_in(xm1[None], xm1, cos[:N_META], sin[:N_META], gm, win, cw, gq, wuqt, gkv, wuk, wuvt,
                                    N_META, scale)
    k_meta, vt_meta = k_meta[0], vt_meta[0, 0]

    def trunk(x):
        bsz, seq, _ = x.shape
        rows = _pick(seq, 512)
        x1 = _ffn1(x.reshape(bsz * seq, D_MODEL), g1, wg1, wu1, wd1, rows)
        yc, qt, k, vt = _mix_in(x1.reshape(bsz, seq, D_MODEL), xm1, cos[N_META:N_META + seq], sin[N_META:N_META + seq],
                                gm, win, cw, gq, wuqt, gkv, wuk, wuvt, rows, scale)
        ya = _attention(qt, k, vt, k_meta, vt_meta, _pick(seq, 512))
        y = _out_ffn2(x1, yc.reshape(bsz * seq, D_CONV), ya.reshape(bsz * seq, D_V), wo, g2, wg2, wu2, wd2, gf, rows)
        return y.reshape(bsz, seq, D_MODEL)

    return (trunk(x_prompt), trunk(x_sample))
```

```python
import functools

import jax
import jax.numpy as jnp
from jax import lax
from jax.experimental import pallas as pl
from jax.experimental.pallas import tpu as pltpu

D_MODEL = 1024
N_META = 16
D_CONV = 512
N_HEADS = 8
QK_NOPE = 64
QK_ROPE = 32
V_HEAD = 64
Q_LORA = 384
KV_LORA = 256
D_FF = 2816
ROPE_BASE = 10000.0
EPS = 1e-6
LOG2_E = 1.4426950408889634

HEAD_PAD = 128
D_QK = N_HEADS * HEAD_PAD
D_V = N_HEADS * V_HEAD
D_Z = 3 * D_CONV + Q_LORA + KV_LORA + HEAD_PAD
D_QT = D_QK + N_HEADS * QK_ROPE
FFN_ROWS = 1024
FF_CHUNK = 256
HALO = 8
BF16_ROWS = 16
MIX_ROW_PARTS = 2
VMEM_LIMIT = 56 * 1024 * 1024
ATTN_KV_BLOCK = 256
ATTN_LOOKAHEAD = 2

BF16 = jnp.bfloat16
F32 = jnp.float32
_NT = (((1,), (1,)), ((), ()))


def _dot(a, b):
    return jnp.dot(a, b, preferred_element_type=F32)


def _rms(x, g):
    ms = jnp.mean(x * x, axis=-1, keepdims=True)
    return x * lax.rsqrt(ms + EPS) * g


def _swiglu_half_step(x, g_ref, wg_ref, wu_ref, wd_ref, h_ref):
    xn = _rms(x, g_ref[...]).astype(BF16)
    for c in range(D_FF // FF_CHUNK):
        cols = slice(c * FF_CHUNK, (c + 1) * FF_CHUNK)
        gate = _dot(xn, wg_ref[:, cols])
        up = _dot(xn, wu_ref[:, cols])
        h_ref[:, cols] = (gate * jax.nn.sigmoid(gate) * up).astype(BF16)
    return x + 0.5 * _dot(h_ref[...], wd_ref[...])


def _ffn1_kernel(x_ref, g_ref, wg_ref, wu_ref, wd_ref, o_ref, h_ref):
    o_ref[...] = _swiglu_half_step(x_ref[...], g_ref, wg_ref, wu_ref, wd_ref, h_ref)


def _out_ffn2_kernel(x_ref, yc_ref, ya_ref, wo_ref, g_ref, wg_ref, wu_ref, wd_ref, gf_ref, o_ref, h_ref):
    x = x_ref[...] + _dot(yc_ref[...], wo_ref[:D_CONV, :]) + _dot(ya_ref[...], wo_ref[D_CONV:, :])
    x = _swiglu_half_step(x, g_ref, wg_ref, wu_ref, wd_ref, h_ref)
    o_ref[...] = _rms(x, gf_ref[...])


def _mix_in_kernel(x_ref, xl_ref, xr_ref, xm_ref, ropek_ref, ropeqt_ref, g_ref, win_ref, cw_ref,
                   gq_ref, wuqt_ref, gkv_ref, wuk_ref, wuvt_ref, yc_ref, qt_ref, k_ref, vt_ref, *, scale):
    t = pl.program_id(1)
    n_t = pl.num_programs(1)
    rows = x_ref.shape[0]
    g = g_ref[...]
    n_parts = MIX_ROW_PARTS if rows % (MIX_ROW_PARTS * HEAD_PAD) == 0 else 1
    parts = [slice(r, r + rows // n_parts) for r in range(0, rows, rows // n_parts)]
    z = jnp.concatenate([_dot(_rms(x_ref[r, :], g).astype(BF16), win_ref[...]) for r in parts], axis=0)
    b_gate = z[:, 0:D_CONV]
    u = z[:, D_CONV:2 * D_CONV] * z[:, 2 * D_CONV:3 * D_CONV]
    o_q = 3 * D_CONV
    o_kv = o_q + Q_LORA
    o_kr = o_kv + KV_LORA

    left = jnp.where(t == 0, xm_ref[N_META - HALO:, :], xl_ref[...])
    xh = jnp.concatenate([left, xr_ref[...]], axis=0)
    zh = _dot(_rms(xh, g).astype(BF16), win_ref[:, D_CONV:3 * D_CONV])
    uh = zh[:, :D_CONV] * zh[:, D_CONV:]
    u_before = uh[HALO - 1:HALO, :]
    u_after = jnp.where(t == n_t - 1, 0.0, uh[HALO:HALO + 1, :])
    row = lax.broadcasted_iota(jnp.int32, (rows, 1), 0)
    u_prev = jnp.where(row == 0, u_before, pltpu.roll(u, 1, 0))
    u_next = jnp.where(row == rows - 1, u_after, pltpu.roll(u, rows - 1, 0))
    cw = cw_ref[...]
    yc_ref[...] = (b_gate * (u_prev * cw[0:1, :] + u * cw[1:2, :] + u_next * cw[2:3, :])).astype(BF16)

    lane = lax.broadcasted_iota(jnp.int32, (1, HEAD_PAD), 1)
    rope_lanes = (lane >= QK_NOPE) & (lane < QK_NOPE + QK_ROPE)
    for r in parts:
        qn = _rms(z[r, o_q:o_kv], gq_ref[...]).astype(BF16)
        qqt = lax.dot_general(wuqt_ref[...], qn, _NT, preferred_element_type=F32)
        kvn = _rms(z[r, o_kv:o_kr], gkv_ref[...]).astype(BF16)
        kk = _dot(kvn, wuk_ref[...])
        vt_ref[:, r] = lax.dot_general(wuvt_ref[...], kvn, _NT, preferred_element_type=F32).astype(BF16)
        kr = z[r, o_kr:] * ropek_ref[r, :]
        kr = kr + pltpu.roll(kr, HEAD_PAD - QK_ROPE, 1)
        k_rope = jnp.where(rope_lanes, pltpu.roll(kr, QK_NOPE, 1), 0.0)
        cost = ropeqt_ref[:QK_ROPE, r]
        sint = ropeqt_ref[QK_ROPE:, r]
        for h in range(N_HEADS):
            cols = slice(h * HEAD_PAD, (h + 1) * HEAD_PAD)
            base = qqt[cols, :]
            rot = qqt[D_QK + h * QK_ROPE:D_QK + (h + 1) * QK_ROPE, :]
            q_rope = base[QK_NOPE:QK_NOPE + QK_ROPE] * cost + rot * sint
            qh = jnp.concatenate([base[:QK_NOPE], q_rope, base[QK_NOPE + QK_ROPE:]], axis=0)
            qt_ref[cols, r] = (qh * scale).astype(BF16)
            k_ref[r, cols] = (kk[:, cols] + k_rope).astype(BF16)


def _attention_kernel(qt_ref, k_ref, vt_ref, km_ref, vmt_ref, o_ref):
    n_kv, _, kv_chunk = vt_ref.shape
    sub = min(ATTN_KV_BLOCK, kv_chunk)
    items = [(h, c, j) for h in range(N_HEADS) for c in range(-1, n_kv)
             for j in range(1 if c < 0 else kv_chunk // sub)]
    last = (n_kv - 1, kv_chunk // sub - 1)

    def scores(item):
        h, c, j = item
        cols = slice(h * HEAD_PAD, (h + 1) * HEAD_PAD)
        kc = km_ref[:, cols] if c < 0 else k_ref[c * kv_chunk + j * sub:c * kv_chunk + (j + 1) * sub, cols]
        s = _dot(kc, qt_ref[cols, :])
        return s, jnp.max(s, axis=0, keepdims=True)

    def weighted_values(item, p):
        h, c, j = item
        vrows = slice(h * V_HEAD, (h + 1) * V_HEAD)
        vtc = vmt_ref[vrows, :] if c < 0 else vt_ref[c, vrows, j * sub:(j + 1) * sub]
        vtc = jnp.concatenate([vtc, jnp.ones((BF16_ROWS, vtc.shape[1]), BF16)], axis=0)
        return _dot(vtc, p)

    ahead = [scores(it) for it in items[:ATTN_LOOKAHEAD]]
    outs, m, acc = [], {}, {}

    for i, item in enumerate(items):
        if i + ATTN_LOOKAHEAD < len(items):
            ahead.append(scores(items[i + ATTN_LOOKAHEAD]))
        s, s_max = ahead.pop(0)
        h = item[0]
        first = item[1] < 0
        m_new = s_max if first else jnp.maximum(m[h], s_max)
        p = jnp.exp2(s - m_new).astype(BF16)
        pv = weighted_values(item, p)
        acc[h] = pv if first else jnp.exp2(m[h] - m_new) * acc[h] + pv
        m[h] = m_new
        if item[1:] == last:
            outs.append(acc[h][:V_HEAD] / acc[h][V_HEAD:V_HEAD + 1])
    o_ref[...] = jnp.concatenate(outs, axis=0).T.astype(BF16)


def _const_spec(shape):
    return pl.BlockSpec(shape, lambda *_: (0,) * len(shape), pipeline_mode=pl.Buffered(1))


def _params(n_grid):
    return pltpu.CompilerParams(dimension_semantics=("arbitrary",) * n_grid, vmem_limit_bytes=VMEM_LIMIT)


def _ffn1(x, g, wg, wu, wd, rows):
    n = x.shape[0]
    return pl.pallas_call(
        _ffn1_kernel,
        grid=(n // rows,),
        in_specs=[pl.BlockSpec((rows, D_MODEL), lambda i: (i, 0)),
                  _const_spec((1, D_MODEL)), _const_spec((D_MODEL, D_FF)), _const_spec((D_MODEL, D_FF)),
                  _const_spec((D_FF, D_MODEL))],
        out_specs=pl.BlockSpec((rows, D_MODEL), lambda i: (i, 0)),
        out_shape=jax.ShapeDtypeStruct((n, D_MODEL), F32),
        scratch_shapes=[pltpu.VMEM((rows, D_FF), BF16)],
        compiler_params=_params(1),
        name="ffn1",
    )(x, g, wg, wu, wd)


def _out_ffn2(x, yc, ya, wo, g, wg, wu, wd, gf, rows):
    n = x.shape[0]
    return pl.pallas_call(
        _out_ffn2_kernel,
        grid=(n // rows,),
        in_specs=[pl.BlockSpec((rows, D_MODEL), lambda i: (i, 0)),
                  pl.BlockSpec((rows, D_CONV), lambda i: (i, 0)),
                  pl.BlockSpec((rows, D_V), lambda i: (i, 0)),
                  _const_spec((D_CONV + D_V, D_MODEL)),
                  _const_spec((1, D_MODEL)), _const_spec((D_MODEL, D_FF)), _const_spec((D_MODEL, D_FF)),
                  _const_spec((D_FF, D_MODEL)), _const_spec((1, D_MODEL))],
        out_specs=pl.BlockSpec((rows, D_MODEL), lambda i: (i, 0)),
        out_shape=jax.ShapeDtypeStruct((n, D_MODEL), F32),
        scratch_shapes=[pltpu.VMEM((rows, D_FF), BF16)],
        compiler_params=_params(1),
        name="out_ffn2",
    )(x, yc, ya, wo, g, wg, wu, wd, gf)


def _mix_in(x, x_meta, ropek, ropeqt, g, win, cw, gq, wuqt, gkv, wuk, wuvt, rows, scale):
    bsz, seq, _ = x.shape
    n_t = seq // rows
    per = rows // HALO
    last = seq // HALO - 1
    tile = lambda d: pl.BlockSpec((None, rows, d), lambda b, t: (b, t, 0))
    return pl.pallas_call(
        functools.partial(_mix_in_kernel, scale=scale),
        grid=(bsz, n_t),
        in_specs=[tile(D_MODEL),
                  pl.BlockSpec((None, HALO, D_MODEL), lambda b, t: (b, jnp.maximum(t * per - 1, 0), 0)),
                  pl.BlockSpec((None, HALO, D_MODEL), lambda b, t: (b, jnp.minimum((t + 1) * per, last), 0)),
                  _const_spec((N_META, D_MODEL)),
                  pl.BlockSpec((rows, HEAD_PAD), lambda b, t: (t, 0)),
                  pl.BlockSpec((2 * QK_ROPE, rows), lambda b, t: (0, t)),
                  _const_spec((1, D_MODEL)), _const_spec((D_MODEL, D_Z)), _const_spec((3, D_CONV)),
                  _const_spec((1, Q_LORA)), _const_spec((D_QT, Q_LORA)),
                  _const_spec((1, KV_LORA)), _const_spec((KV_LORA, D_QK)), _const_spec((D_V, KV_LORA))],
        out_specs=[tile(D_CONV), pl.BlockSpec((None, D_QK, rows), lambda b, t: (b, 0, t)), tile(D_QK),
                   pl.BlockSpec((None, None, D_V, rows), lambda b, t: (b, t, 0, 0))],
        out_shape=[jax.ShapeDtypeStruct((bsz, seq, D_CONV), BF16),
                   jax.ShapeDtypeStruct((bsz, D_QK, seq), BF16),
                   jax.ShapeDtypeStruct((bsz, seq, D_QK), BF16),
                   jax.ShapeDtypeStruct((bsz, n_t, D_V, rows), BF16)],
        compiler_params=_params(2),
        name="mix_in",
    )(x, x, x, x_meta, ropek, ropeqt, g, win, cw, gq, wuqt, gkv, wuk, wuvt)


def _attention(qt, k, vt, k_meta, vt_meta, q_block):
    bsz, _, seq = qt.shape
    _, n_kv, _, kv_block = vt.shape
    return pl.pallas_call(
        _attention_kernel,
        grid=(bsz, seq // q_block),
        in_specs=[pl.BlockSpec((None, D_QK, q_block), lambda b, t: (b, 0, t)),
                  pl.BlockSpec((None, seq, D_QK), lambda b, t: (b, 0, 0)),
                  pl.BlockSpec((None, n_kv, D_V, kv_block), lambda b, t: (b, 0, 0, 0)),
                  _const_spec((N_META, D_QK)), _const_spec((D_V, N_META))],
        out_specs=pl.BlockSpec((None, q_block, D_V), lambda b, t: (b, t, 0)),
        out_shape=jax.ShapeDtypeStruct((bsz, seq, D_V), BF16),
        compiler_params=_params(2),
        name="attention",
    )(qt, k, vt, k_meta, vt_meta)


def _rot_half(w):
    half = QK_ROPE // 2
    return jnp.concatenate([-w[..., half:], w[..., :half]], axis=-1)


def _prepare_weights(w_in, w_uq, w_ukv):
    w_kr = w_in[:, 3 * D_CONV + Q_LORA + KV_LORA:]
    win = jnp.concatenate([w_in, _rot_half(w_kr), jnp.zeros((D_MODEL, HEAD_PAD - 2 * QK_ROPE), F32)], axis=1)
    wq = w_uq.reshape(Q_LORA, N_HEADS, QK_NOPE + QK_ROPE)
    zq = jnp.zeros((Q_LORA, N_HEADS, HEAD_PAD - QK_NOPE - QK_ROPE), F32)
    q_base = jnp.concatenate([wq, zq], axis=-1).reshape(Q_LORA, D_QK)
    q_rot = _rot_half(wq[..., QK_NOPE:]).reshape(Q_LORA, N_HEADS * QK_ROPE)
    wuqt = jnp.concatenate([q_base, q_rot], axis=1).T
    wkv = w_ukv.reshape(KV_LORA, N_HEADS, QK_NOPE + V_HEAD)
    wuk = jnp.concatenate([wkv[..., :QK_NOPE], jnp.zeros((KV_LORA, N_HEADS, HEAD_PAD - QK_NOPE), F32)],
                          axis=-1).reshape(KV_LORA, D_QK)
    wuvt = wkv[..., QK_NOPE:].reshape(KV_LORA, D_V).T
    return win.astype(BF16), wuqt.astype(BF16), wuk.astype(BF16), wuvt.astype(BF16)


def _rope_tables(n_pos):
    pos = jnp.arange(n_pos, dtype=F32)
    inv_freq = 1.0 / (ROPE_BASE ** (jnp.arange(0, QK_ROPE, 2, dtype=F32) / QK_ROPE))
    ang = pos[:, None] * inv_freq[None, :]
    cos, sin = jnp.cos(ang), jnp.sin(ang)
    table = jnp.concatenate([cos, cos, sin, sin], axis=1)
    return jnp.pad(table, ((0, 0), (0, HEAD_PAD - 2 * QK_ROPE))), table.T


def _pick(n, target):
    while n % target:
        target //= 2
    return target


def kernel(x_prompt, x_sample, meta_tokens, ffn1_norm, ffn1_w_gate, ffn1_w_up, ffn1_w_down, mix_norm, w_in, conv_w, q_norm, w_uq, kv_norm, w_ukv, w_out, ffn2_norm, ffn2_w_gate, ffn2_w_up, ffn2_w_down, final_norm):
    assert ffn1_norm.shape[0] == 1, "single layer: meta-token outputs are only consumed as keys/values"
    scale = (QK_NOPE + QK_ROPE) ** -0.5 * LOG2_E
    g1, g2, gm = ffn1_norm, ffn2_norm, mix_norm
    gq, gkv, gf = q_norm, kv_norm, final_norm.reshape(1, D_MODEL)
    wg1, wu1, wd1 = ffn1_w_gate[0].astype(BF16), ffn1_w_up[0].astype(BF16), ffn1_w_down[0].astype(BF16)
    wg2, wu2, wd2 = ffn2_w_gate[0].astype(BF16), ffn2_w_up[0].astype(BF16), ffn2_w_down[0].astype(BF16)
    wo = w_out[0].astype(BF16)
    win, wuqt, wuk, wuvt = _prepare_weights(w_in[0], w_uq[0], w_ukv[0])
    cw = conv_w[0]

    max_seq = max(x_prompt.shape[1], x_sample.shape[1])
    ropek, ropeqt = _rope_tables(N_META + max_seq)

    xm1 = _ffn1(meta_tokens, g1, wg1, wu1, wd1, N_META)
    _, _, k_meta, vt_meta = _mix_in(xm1[None], xm1, ropek[:N_META], ropeqt[:, :N_META], gm, win, cw, gq, wuqt, gkv, wuk, wuvt,
                                    N_META, scale)
    k_meta, vt_meta = k_meta[0], vt_meta[0, 0]

    def trunk(x):
        bsz, seq, _ = x.shape
        rows = _pick(seq, 512)
        ffn_rows = _pick(bsz * seq, FFN_ROWS)
        x1 = _ffn1(x.reshape(bsz * seq, D_MODEL), g1, wg1, wu1, wd1, ffn_rows)
        yc, qt, k, vt = _mix_in(x1.reshape(bsz, seq, D_MODEL), xm1, ropek[N_META:N_META + seq], ropeqt[:, N_META:N_META + seq],
                                gm, win, cw, gq, wuqt, gkv, wuk, wuvt, rows, scale)
        ya = _attention(qt, k, vt, k_meta, vt_meta, _pick(seq, 512))
        y = _out_ffn2(x1, yc.reshape(bsz * seq, D_CONV), ya.reshape(bsz * seq, D_V), wo, g2, wg2, wu2, wd2, gf, ffn_rows)
        return y.reshape(bsz, seq, D_MODEL)

    return (trunk(x_prompt), trunk(x_sample))
```

```python
import functools

import jax
import jax.numpy as jnp
from jax import lax
from jax.experimental import pallas as pl
from jax.experimental.pallas import tpu as pltpu

D_MODEL = 1024
N_META = 16
D_CONV = 512
N_HEADS = 8
QK_NOPE = 64
QK_ROPE = 32
V_HEAD = 64
Q_LORA = 384
KV_LORA = 256
D_FF = 2816
ROPE_BASE = 10000.0
EPS = 1e-6
LOG2_E = 1.4426950408889634

HEAD_PAD = 128
D_QK = N_HEADS * HEAD_PAD
D_V = N_HEADS * V_HEAD
D_Z = 3 * D_CONV + Q_LORA + KV_LORA + HEAD_PAD
D_QT = D_QK + N_HEADS * QK_ROPE
FFN_ROWS = 1024
FF_CHUNK = 256
HALO = 8
BF16_ROWS = 16
MIX_ROW_PARTS = 2
VMEM_LIMIT = 56 * 1024 * 1024
ATTN_KV_BLOCK = 256
ATTN_LOOKAHEAD = 2

BF16 = jnp.bfloat16
F32 = jnp.float32
_NT = (((1,), (1,)), ((), ()))


def _dot(a, b):
    return jnp.dot(a, b, preferred_element_type=F32)


def _rms(x, g):
    ms = jnp.mean(x * x, axis=-1, keepdims=True)
    return x * lax.rsqrt(ms + EPS) * g


def _swiglu_half_step(x, g_ref, wg_ref, wu_ref, wd_ref, h_ref):
    xn = _rms(x, g_ref[...]).astype(BF16)
    for c in range(D_FF // FF_CHUNK):
        cols = slice(c * FF_CHUNK, (c + 1) * FF_CHUNK)
        gate = _dot(xn, wg_ref[:, cols])
        up = _dot(xn, wu_ref[:, cols])
        h_ref[:, cols] = (gate * jax.nn.sigmoid(gate) * up).astype(BF16)
    return x + 0.5 * _dot(h_ref[...], wd_ref[...])


def _ffn1_kernel(x_ref, g_ref, wg_ref, wu_ref, wd_ref, o_ref, h_ref):
    o_ref[...] = _swiglu_half_step(x_ref[...], g_ref, wg_ref, wu_ref, wd_ref, h_ref)


def _out_ffn2_kernel(x_ref, yc_ref, ya_ref, wo_ref, g_ref, wg_ref, wu_ref, wd_ref, gf_ref, o_ref, h_ref):
    x = x_ref[...] + _dot(yc_ref[...], wo_ref[:D_CONV, :]) + _dot(ya_ref[...], wo_ref[D_CONV:, :])
    x = _swiglu_half_step(x, g_ref, wg_ref, wu_ref, wd_ref, h_ref)
    o_ref[...] = _rms(x, gf_ref[...])


def _mix_in_kernel(x_ref, xl_ref, xr_ref, xm_ref, ropek_ref, ropeqt_ref, g_ref, win_ref, cw_ref,
                   gq_ref, wuqt_ref, gkv_ref, wuk_ref, wuvt_ref, yc_ref, qt_ref, k_ref, vt_ref, *, scale):
    t = pl.program_id(1)
    n_t = pl.num_programs(1)
    rows = x_ref.shape[0]
    g = g_ref[...]
    n_parts = MIX_ROW_PARTS if rows % (MIX_ROW_PARTS * HEAD_PAD) == 0 else 1
    parts = [slice(r, r + rows // n_parts) for r in range(0, rows, rows // n_parts)]
    z = jnp.concatenate([_dot(_rms(x_ref[r, :], g).astype(BF16), win_ref[...]) for r in parts], axis=0)
    b_gate = z[:, 0:D_CONV]
    u = z[:, D_CONV:2 * D_CONV] * z[:, 2 * D_CONV:3 * D_CONV]
    o_q = 3 * D_CONV
    o_kv = o_q + Q_LORA
    o_kr = o_kv + KV_LORA

    left = jnp.where(t == 0, xm_ref[N_META - HALO:, :], xl_ref[...])
    xh = jnp.concatenate([left, xr_ref[...]], axis=0)
    zh = _dot(_rms(xh, g).astype(BF16), win_ref[:, D_CONV:3 * D_CONV])
    uh = zh[:, :D_CONV] * zh[:, D_CONV:]
    u_before = uh[HALO - 1:HALO, :]
    u_after = jnp.where(t == n_t - 1, 0.0, uh[HALO:HALO + 1, :])
    row = lax.broadcasted_iota(jnp.int32, (rows, 1), 0)
    u_prev = jnp.where(row == 0, u_before, pltpu.roll(u, 1, 0))
    u_next = jnp.where(row == rows - 1, u_after, pltpu.roll(u, rows - 1, 0))
    cw = cw_ref[...]
    yc_ref[...] = (b_gate * (u_prev * cw[0:1, :] + u * cw[1:2, :] + u_next * cw[2:3, :])).astype(BF16)

    lane = lax.broadcasted_iota(jnp.int32, (1, HEAD_PAD), 1)
    rope_lanes = (lane >= QK_NOPE) & (lane < QK_NOPE + QK_ROPE)
    for r in parts:
        qn = _rms(z[r, o_q:o_kv], gq_ref[...]).astype(BF16)
        qqt = lax.dot_general(wuqt_ref[...], qn, _NT, preferred_element_type=F32)
        kvn = _rms(z[r, o_kv:o_kr], gkv_ref[...]).astype(BF16)
        kk = _dot(kvn, wuk_ref[...])
        vt_ref[:, r] = lax.dot_general(wuvt_ref[...], kvn, _NT, preferred_element_type=F32).astype(BF16)
        kr = z[r, o_kr:] * ropek_ref[r, :]
        kr = kr + pltpu.roll(kr, HEAD_PAD - QK_ROPE, 1)
        k_rope = jnp.where(rope_lanes, pltpu.roll(kr, QK_NOPE, 1), 0.0)
        cost = ropeqt_ref[:QK_ROPE, r]
        sint = ropeqt_ref[QK_ROPE:, r]
        for h in range(N_HEADS):
            cols = slice(h * HEAD_PAD, (h + 1) * HEAD_PAD)
            base = qqt[cols, :]
            rot = qqt[D_QK + h * QK_ROPE:D_QK + (h + 1) * QK_ROPE, :]
            q_rope = base[QK_NOPE:QK_NOPE + QK_ROPE] * cost + rot * sint
            qh = jnp.concatenate([base[:QK_NOPE], q_rope, base[QK_NOPE + QK_ROPE:]], axis=0)
            qt_ref[cols, r] = (qh * scale).astype(BF16)
            k_ref[h, r, :] = (kk[:, cols] + k_rope).astype(BF16)


def _attention_kernel(qt_ref, k_ref, vt_ref, km_ref, vmt_ref, o_ref):
    n_kv, _, kv_chunk = vt_ref.shape
    sub = min(ATTN_KV_BLOCK, kv_chunk)
    blocks = [(c, j) for c in range(n_kv) for j in range(kv_chunk // sub)]
    steps = [(h, bl) for h in range(N_HEADS) for bl in [((-1, 0), blocks[0])] + [(b,) for b in blocks[1:]]]

    def scores(step):
        h, bl = step
        cols = slice(h * HEAD_PAD, (h + 1) * HEAD_PAD)
        out = []
        for c, j in bl:
            kc = km_ref[h] if c < 0 else k_ref[h, c * kv_chunk + j * sub:c * kv_chunk + (j + 1) * sub, :]
            out.append(_dot(kc, qt_ref[cols, :]))
        s_max = functools.reduce(jnp.maximum, [jnp.max(s, axis=0, keepdims=True) for s in out])
        return out, s_max

    def weighted_values(h, block, p):
        c, j = block
        vrows = slice(h * V_HEAD, (h + 1) * V_HEAD)
        vtc = vmt_ref[vrows, :] if c < 0 else vt_ref[c, vrows, j * sub:(j + 1) * sub]
        vtc = jnp.concatenate([vtc, jnp.ones((BF16_ROWS, vtc.shape[1]), BF16)], axis=0)
        return _dot(vtc, p)

    ahead = [scores(st) for st in steps[:ATTN_LOOKAHEAD]]
    outs, m, acc = [], None, None
    for i, (h, bl) in enumerate(steps):
        if i + ATTN_LOOKAHEAD < len(steps):
            ahead.append(scores(steps[i + ATTN_LOOKAHEAD]))
        ss, s_max = ahead.pop(0)
        first = bl[0][0] < 0
        m_new = s_max if first else jnp.maximum(m, s_max)
        pv = sum(weighted_values(h, b, jnp.exp2(s - m_new).astype(BF16)) for b, s in zip(bl, ss))
        acc = pv if first else jnp.exp2(m - m_new) * acc + pv
        m = m_new
        if bl[-1] == blocks[-1]:
            outs.append(acc[:V_HEAD] / acc[V_HEAD:V_HEAD + 1])
    o_ref[...] = jnp.concatenate(outs, axis=0).T.astype(BF16)


def _const_spec(shape):
    return pl.BlockSpec(shape, lambda *_: (0,) * len(shape), pipeline_mode=pl.Buffered(1))


def _params(n_grid):
    return pltpu.CompilerParams(dimension_semantics=("arbitrary",) * n_grid, vmem_limit_bytes=VMEM_LIMIT)


def _ffn1(x, g, wg, wu, wd, rows):
    n = x.shape[0]
    return pl.pallas_call(
        _ffn1_kernel,
        grid=(n // rows,),
        in_specs=[pl.BlockSpec((rows, D_MODEL), lambda i: (i, 0)),
                  _const_spec((1, D_MODEL)), _const_spec((D_MODEL, D_FF)), _const_spec((D_MODEL, D_FF)),
                  _const_spec((D_FF, D_MODEL))],
        out_specs=pl.BlockSpec((rows, D_MODEL), lambda i: (i, 0)),
        out_shape=jax.ShapeDtypeStruct((n, D_MODEL), F32),
        scratch_shapes=[pltpu.VMEM((rows, D_FF), BF16)],
        compiler_params=_params(1),
        name="ffn1",
    )(x, g, wg, wu, wd)


def _out_ffn2(x, yc, ya, wo, g, wg, wu, wd, gf, rows):
    n = x.shape[0]
    return pl.pallas_call(
        _out_ffn2_kernel,
        grid=(n // rows,),
        in_specs=[pl.BlockSpec((rows, D_MODEL), lambda i: (i, 0)),
                  pl.BlockSpec((rows, D_CONV), lambda i: (i, 0)),
                  pl.BlockSpec((rows, D_V), lambda i: (i, 0)),
                  _const_spec((D_CONV + D_V, D_MODEL)),
                  _const_spec((1, D_MODEL)), _const_spec((D_MODEL, D_FF)), _const_spec((D_MODEL, D_FF)),
                  _const_spec((D_FF, D_MODEL)), _const_spec((1, D_MODEL))],
        out_specs=pl.BlockSpec((rows, D_MODEL), lambda i: (i, 0)),
        out_shape=jax.ShapeDtypeStruct((n, D_MODEL), F32),
        scratch_shapes=[pltpu.VMEM((rows, D_FF), BF16)],
        compiler_params=_params(1),
        name="out_ffn2",
    )(x, yc, ya, wo, g, wg, wu, wd, gf)


def _mix_in(x, x_meta, ropek, ropeqt, g, win, cw, gq, wuqt, gkv, wuk, wuvt, rows, scale):
    bsz, seq, _ = x.shape
    n_t = seq // rows
    per = rows // HALO
    last = seq // HALO - 1
    tile = lambda d: pl.BlockSpec((None, rows, d), lambda b, t: (b, t, 0))
    return pl.pallas_call(
        functools.partial(_mix_in_kernel, scale=scale),
        grid=(bsz, n_t),
        in_specs=[tile(D_MODEL),
                  pl.BlockSpec((None, HALO, D_MODEL), lambda b, t: (b, jnp.maximum(t * per - 1, 0), 0)),
                  pl.BlockSpec((None, HALO, D_MODEL), lambda b, t: (b, jnp.minimum((t + 1) * per, last), 0)),
                  _const_spec((N_META, D_MODEL)),
                  pl.BlockSpec((rows, HEAD_PAD), lambda b, t: (t, 0)),
                  pl.BlockSpec((2 * QK_ROPE, rows), lambda b, t: (0, t)),
                  _const_spec((1, D_MODEL)), _const_spec((D_MODEL, D_Z)), _const_spec((3, D_CONV)),
                  _const_spec((1, Q_LORA)), _const_spec((D_QT, Q_LORA)),
                  _const_spec((1, KV_LORA)), _const_spec((KV_LORA, D_QK)), _const_spec((D_V, KV_LORA))],
        out_specs=[tile(D_CONV), pl.BlockSpec((None, D_QK, rows), lambda b, t: (b, 0, t)),
                   pl.BlockSpec((None, N_HEADS, rows, HEAD_PAD), lambda b, t: (b, 0, t, 0)),
                   pl.BlockSpec((None, None, D_V, rows), lambda b, t: (b, t, 0, 0))],
        out_shape=[jax.ShapeDtypeStruct((bsz, seq, D_CONV), BF16),
                   jax.ShapeDtypeStruct((bsz, D_QK, seq), BF16),
                   jax.ShapeDtypeStruct((bsz, N_HEADS, seq, HEAD_PAD), BF16),
                   jax.ShapeDtypeStruct((bsz, n_t, D_V, rows), BF16)],
        compiler_params=_params(2),
        name="mix_in",
    )(x, x, x, x_meta, ropek, ropeqt, g, win, cw, gq, wuqt, gkv, wuk, wuvt)


def _attention(qt, k, vt, k_meta, vt_meta, q_block):
    bsz, _, seq = qt.shape
    _, n_kv, _, kv_block = vt.shape
    return pl.pallas_call(
        _attention_kernel,
        grid=(bsz, seq // q_block),
        in_specs=[pl.BlockSpec((None, D_QK, q_block), lambda b, t: (b, 0, t)),
                  pl.BlockSpec((None, N_HEADS, seq, HEAD_PAD), lambda b, t: (b, 0, 0, 0)),
                  pl.BlockSpec((None, n_kv, D_V, kv_block), lambda b, t: (b, 0, 0, 0)),
                  _const_spec((N_HEADS, N_META, HEAD_PAD)), _const_spec((D_V, N_META))],
        out_specs=pl.BlockSpec((None, q_block, D_V), lambda b, t: (b, t, 0)),
        out_shape=jax.ShapeDtypeStruct((bsz, seq, D_V), BF16),
        compiler_params=_params(2),
        name="attention",
    )(qt, k, vt, k_meta, vt_meta)


def _rot_half(w):
    half = QK_ROPE // 2
    return jnp.concatenate([-w[..., half:], w[..., :half]], axis=-1)


def _prepare_weights(w_in, w_uq, w_ukv):
    w_kr = w_in[:, 3 * D_CONV + Q_LORA + KV_LORA:]
    win = jnp.concatenate([w_in, _rot_half(w_kr), jnp.zeros((D_MODEL, HEAD_PAD - 2 * QK_ROPE), F32)], axis=1)
    wq = w_uq.reshape(Q_LORA, N_HEADS, QK_NOPE + QK_ROPE)
    zq = jnp.zeros((Q_LORA, N_HEADS, HEAD_PAD - QK_NOPE - QK_ROPE), F32)
    q_base = jnp.concatenate([wq, zq], axis=-1).reshape(Q_LORA, D_QK)
    q_rot = _rot_half(wq[..., QK_NOPE:]).reshape(Q_LORA, N_HEADS * QK_ROPE)
    wuqt = jnp.concatenate([q_base, q_rot], axis=1).T
    wkv = w_ukv.reshape(KV_LORA, N_HEADS, QK_NOPE + V_HEAD)
    wuk = jnp.concatenate([wkv[..., :QK_NOPE], jnp.zeros((KV_LORA, N_HEADS, HEAD_PAD - QK_NOPE), F32)],
                          axis=-1).reshape(KV_LORA, D_QK)
    wuvt = wkv[..., QK_NOPE:].reshape(KV_LORA, D_V).T
    return win.astype(BF16), wuqt.astype(BF16), wuk.astype(BF16), wuvt.astype(BF16)


def _rope_tables(n_pos):
    pos = jnp.arange(n_pos, dtype=F32)
    inv_freq = 1.0 / (ROPE_BASE ** (jnp.arange(0, QK_ROPE, 2, dtype=F32) / QK_ROPE))
    ang = pos[:, None] * inv_freq[None, :]
    cos, sin = jnp.cos(ang), jnp.sin(ang)
    table = jnp.concatenate([cos, cos, sin, sin], axis=1)
    return jnp.pad(table, ((0, 0), (0, HEAD_PAD - 2 * QK_ROPE))), table.T


def _pick(n, target):
    while n % target:
        target //= 2
    return target


def kernel(x_prompt, x_sample, meta_tokens, ffn1_norm, ffn1_w_gate, ffn1_w_up, ffn1_w_down, mix_norm, w_in, conv_w, q_norm, w_uq, kv_norm, w_ukv, w_out, ffn2_norm, ffn2_w_gate, ffn2_w_up, ffn2_w_down, final_norm):
    assert ffn1_norm.shape[0] == 1, "single layer: meta-token outputs are only consumed as keys/values"
    scale = (QK_NOPE + QK_ROPE) ** -0.5 * LOG2_E
    g1, g2, gm = ffn1_norm, ffn2_norm, mix_norm
    gq, gkv, gf = q_norm, kv_norm, final_norm.reshape(1, D_MODEL)
    wg1, wu1, wd1 = ffn1_w_gate[0].astype(BF16), ffn1_w_up[0].astype(BF16), ffn1_w_down[0].astype(BF16)
    wg2, wu2, wd2 = ffn2_w_gate[0].astype(BF16), ffn2_w_up[0].astype(BF16), ffn2_w_down[0].astype(BF16)
    wo = w_out[0].astype(BF16)
    win, wuqt, wuk, wuvt = _prepare_weights(w_in[0], w_uq[0], w_ukv[0])
    cw = conv_w[0]

    max_seq = max(x_prompt.shape[1], x_sample.shape[1])
    ropek, ropeqt = _rope_tables(N_META + max_seq)

    xm1 = _ffn1(meta_tokens, g1, wg1, wu1, wd1, N_META)
    _, _, k_meta, vt_meta = _mix_in(xm1[None], xm1, ropek[:N_META], ropeqt[:, :N_META], gm, win, cw, gq, wuqt, gkv, wuk, wuvt,
                                    N_META, scale)
    k_meta, vt_meta = k_meta[0], vt_meta[0, 0]

    def trunk(x):
        bsz, seq, _ = x.shape
        rows = _pick(seq, 512)
        ffn_rows = _pick(bsz * seq, FFN_ROWS)
        x1 = _ffn1(x.reshape(bsz * seq, D_MODEL), g1, wg1, wu1, wd1, ffn_rows)
        yc, qt, k, vt = _mix_in(x1.reshape(bsz, seq, D_MODEL), xm1, ropek[N_META:N_META + seq], ropeqt[:, N_META:N_META + seq],
                                gm, win, cw, gq, wuqt, gkv, wuk, wuvt, rows, scale)
        ya = _attention(qt, k, vt, k_meta, vt_meta, _pick(seq, 512))
        y = _out_ffn2(x1, yc.reshape(bsz * seq, D_CONV), ya.reshape(bsz * seq, D_V), wo, g2, wg2, wu2, wd2, gf, ffn_rows)
        return y.reshape(bsz, seq, D_MODEL)

    return (trunk(x_prompt), trunk(x_sample))
```

```python
import functools

import jax
import jax.numpy as jnp
from jax import lax
from jax.experimental import pallas as pl
from jax.experimental.pallas import tpu as pltpu

D_MODEL = 1024
N_META = 16
D_CONV = 512
N_HEADS = 8
QK_NOPE = 64
QK_ROPE = 32
V_HEAD = 64
Q_LORA = 384
KV_LORA = 256
D_FF = 2816
ROPE_BASE = 10000.0
EPS = 1e-6
LOG2_E = 1.4426950408889634

HEAD_PAD = 128
D_QK = N_HEADS * HEAD_PAD
D_V = N_HEADS * V_HEAD
D_Z = 3 * D_CONV + Q_LORA + KV_LORA + HEAD_PAD
D_QT = D_QK + N_HEADS * QK_ROPE
FFN_ROWS = 1024
FF_CHUNK = 256
HALO = 8
BF16_ROWS = 16
MIX_ROW_PARTS = 4
VT_CHUNK = 512
VMEM_LIMIT = 56 * 1024 * 1024
ATTN_KV_BLOCK = 256
ATTN_LOOKAHEAD = 2

BF16 = jnp.bfloat16
F32 = jnp.float32
_NT = (((1,), (1,)), ((), ()))


def _dot(a, b):
    return jnp.dot(a, b, preferred_element_type=F32)


def _rms(x, g):
    ms = jnp.mean(x * x, axis=-1, keepdims=True)
    return x * lax.rsqrt(ms + EPS) * g


def _swiglu_half_step(x, g_ref, wg_ref, wu_ref, wd_ref, h_ref):
    xn = _rms(x, g_ref[...]).astype(BF16)
    for c in range(D_FF // FF_CHUNK):
        cols = slice(c * FF_CHUNK, (c + 1) * FF_CHUNK)
        gate = _dot(xn, wg_ref[:, cols])
        up = _dot(xn, wu_ref[:, cols])
        h_ref[:, cols] = (gate * jax.nn.sigmoid(gate) * up).astype(BF16)
    return x + 0.5 * _dot(h_ref[...], wd_ref[...])


def _ffn1_kernel(x_ref, g_ref, wg_ref, wu_ref, wd_ref, o_ref, h_ref):
    o_ref[...] = _swiglu_half_step(x_ref[...], g_ref, wg_ref, wu_ref, wd_ref, h_ref)


def _out_ffn2_kernel(x_ref, yc_ref, ya_ref, wo_ref, g_ref, wg_ref, wu_ref, wd_ref, gf_ref, o_ref, h_ref):
    x = x_ref[...] + _dot(yc_ref[...], wo_ref[:D_CONV, :]) + _dot(ya_ref[...], wo_ref[D_CONV:, :])
    x = _swiglu_half_step(x, g_ref, wg_ref, wu_ref, wd_ref, h_ref)
    o_ref[...] = _rms(x, gf_ref[...])


def _mix_in_kernel(x_ref, xl_ref, xr_ref, xm_ref, ropek_ref, ropeqt_ref, g_ref, win_ref, cw_ref,
                   gq_ref, wuqt_ref, gkv_ref, wuk_ref, wuvt_ref, yc_ref, qt_ref, k_ref, vt_ref, *, scale):
    t = pl.program_id(1)
    n_t = pl.num_programs(1)
    rows = x_ref.shape[0]
    g = g_ref[...]
    n_parts = MIX_ROW_PARTS if rows % (MIX_ROW_PARTS * HEAD_PAD) == 0 else 1
    parts = [slice(r, r + rows // n_parts) for r in range(0, rows, rows // n_parts)]
    z = jnp.concatenate([_dot(_rms(x_ref[r, :], g).astype(BF16), win_ref[...]) for r in parts], axis=0)
    b_gate = z[:, 0:D_CONV]
    u = z[:, D_CONV:2 * D_CONV] * z[:, 2 * D_CONV:3 * D_CONV]
    o_q = 3 * D_CONV
    o_kv = o_q + Q_LORA
    o_kr = o_kv + KV_LORA

    left = jnp.where(t == 0, xm_ref[N_META - HALO:, :], xl_ref[...])
    xh = jnp.concatenate([left, xr_ref[...]], axis=0)
    zh = _dot(_rms(xh, g).astype(BF16), win_ref[:, D_CONV:3 * D_CONV])
    uh = zh[:, :D_CONV] * zh[:, D_CONV:]
    u_before = uh[HALO - 1:HALO, :]
    u_after = jnp.where(t == n_t - 1, 0.0, uh[HALO:HALO + 1, :])
    row = lax.broadcasted_iota(jnp.int32, (rows, 1), 0)
    u_prev = jnp.where(row == 0, u_before, pltpu.roll(u, 1, 0))
    u_next = jnp.where(row == rows - 1, u_after, pltpu.roll(u, rows - 1, 0))
    cw = cw_ref[...]
    yc_ref[...] = (b_gate * (u_prev * cw[0:1, :] + u * cw[1:2, :] + u_next * cw[2:3, :])).astype(BF16)

    lane = lax.broadcasted_iota(jnp.int32, (1, HEAD_PAD), 1)
    rope_lanes = (lane >= QK_NOPE) & (lane < QK_NOPE + QK_ROPE)
    for r in parts:
        qn = _rms(z[r, o_q:o_kv], gq_ref[...]).astype(BF16)
        qqt = lax.dot_general(wuqt_ref[...], qn, _NT, preferred_element_type=F32)
        kvn = _rms(z[r, o_kv:o_kr], gkv_ref[...]).astype(BF16)
        kk = _dot(kvn, wuk_ref[...])
        vt = lax.dot_general(wuvt_ref[...], kvn, _NT, preferred_element_type=F32).astype(BF16)
        chunk = vt_ref.shape[-1]
        vt_ref[r.start // chunk, :, r.start % chunk:r.start % chunk + vt.shape[1]] = vt
        kr = z[r, o_kr:] * ropek_ref[r, :]
        kr = kr + pltpu.roll(kr, HEAD_PAD - QK_ROPE, 1)
        k_rope = jnp.where(rope_lanes, pltpu.roll(kr, QK_NOPE, 1), 0.0)
        cost = ropeqt_ref[:QK_ROPE, r]
        sint = ropeqt_ref[QK_ROPE:, r]
        for h in range(N_HEADS):
            cols = slice(h * HEAD_PAD, (h + 1) * HEAD_PAD)
            base = qqt[cols, :]
            rot = qqt[D_QK + h * QK_ROPE:D_QK + (h + 1) * QK_ROPE, :]
            q_rope = base[QK_NOPE:QK_NOPE + QK_ROPE] * cost + rot * sint
            qh = jnp.concatenate([base[:QK_NOPE], q_rope, base[QK_NOPE + QK_ROPE:]], axis=0)
            qt_ref[cols, r] = (qh * scale).astype(BF16)
            k_ref[h, r, :] = (kk[:, cols] + k_rope).astype(BF16)


def _attention_kernel(qt_ref, k_ref, vt_ref, km_ref, vmt_ref, o_ref):
    n_kv, _, kv_chunk = vt_ref.shape
    sub = min(ATTN_KV_BLOCK, kv_chunk)
    blocks = [(c, j) for c in range(n_kv) for j in range(kv_chunk // sub)]
    steps = [(h, bl) for h in range(N_HEADS) for bl in [((-1, 0), blocks[0])] + [(b,) for b in blocks[1:]]]

    def scores(step):
        h, bl = step
        cols = slice(h * HEAD_PAD, (h + 1) * HEAD_PAD)
        out = []
        for c, j in bl:
            kc = km_ref[h] if c < 0 else k_ref[h, c * kv_chunk + j * sub:c * kv_chunk + (j + 1) * sub, :]
            out.append(_dot(kc, qt_ref[cols, :]))
        s_max = functools.reduce(jnp.maximum, [jnp.max(s, axis=0, keepdims=True) for s in out])
        return out, s_max

    def weighted_values(h, block, p):
        c, j = block
        vrows = slice(h * V_HEAD, (h + 1) * V_HEAD)
        vtc = vmt_ref[vrows, :] if c < 0 else vt_ref[c, vrows, j * sub:(j + 1) * sub]
        vtc = jnp.concatenate([vtc, jnp.ones((BF16_ROWS, vtc.shape[1]), BF16)], axis=0)
        return _dot(vtc, p)

    ahead = [scores(st) for st in steps[:ATTN_LOOKAHEAD]]
    outs, m, acc = [], None, None
    for i, (h, bl) in enumerate(steps):
        if i + ATTN_LOOKAHEAD < len(steps):
            ahead.append(scores(steps[i + ATTN_LOOKAHEAD]))
        ss, s_max = ahead.pop(0)
        first = bl[0][0] < 0
        m_new = s_max if first else jnp.maximum(m, s_max)
        pv = sum(weighted_values(h, b, jnp.exp2(s - m_new).astype(BF16)) for b, s in zip(bl, ss))
        acc = pv if first else jnp.exp2(m - m_new) * acc + pv
        m = m_new
        if bl[-1] == blocks[-1]:
            outs.append(acc[:V_HEAD] / acc[V_HEAD:V_HEAD + 1])
    o_ref[...] = jnp.concatenate(outs, axis=0).T.astype(BF16)


def _const_spec(shape):
    return pl.BlockSpec(shape, lambda *_: (0,) * len(shape), pipeline_mode=pl.Buffered(1))


def _params(n_grid):
    return pltpu.CompilerParams(dimension_semantics=("arbitrary",) * n_grid, vmem_limit_bytes=VMEM_LIMIT)


def _ffn1(x, g, wg, wu, wd, rows):
    n = x.shape[0]
    return pl.pallas_call(
        _ffn1_kernel,
        grid=(n // rows,),
        in_specs=[pl.BlockSpec((rows, D_MODEL), lambda i: (i, 0)),
                  _const_spec((1, D_MODEL)), _const_spec((D_MODEL, D_FF)), _const_spec((D_MODEL, D_FF)),
                  _const_spec((D_FF, D_MODEL))],
        out_specs=pl.BlockSpec((rows, D_MODEL), lambda i: (i, 0)),
        out_shape=jax.ShapeDtypeStruct((n, D_MODEL), F32),
        scratch_shapes=[pltpu.VMEM((rows, D_FF), BF16)],
        compiler_params=_params(1),
        name="ffn1",
    )(x, g, wg, wu, wd)


def _out_ffn2(x, yc, ya, wo, g, wg, wu, wd, gf, rows):
    n = x.shape[0]
    return pl.pallas_call(
        _out_ffn2_kernel,
        grid=(n // rows,),
        in_specs=[pl.BlockSpec((rows, D_MODEL), lambda i: (i, 0)),
                  pl.BlockSpec((rows, D_CONV), lambda i: (i, 0)),
                  pl.BlockSpec((rows, D_V), lambda i: (i, 0)),
                  _const_spec((D_CONV + D_V, D_MODEL)),
                  _const_spec((1, D_MODEL)), _const_spec((D_MODEL, D_FF)), _const_spec((D_MODEL, D_FF)),
                  _const_spec((D_FF, D_MODEL)), _const_spec((1, D_MODEL))],
        out_specs=pl.BlockSpec((rows, D_MODEL), lambda i: (i, 0)),
        out_shape=jax.ShapeDtypeStruct((n, D_MODEL), F32),
        scratch_shapes=[pltpu.VMEM((rows, D_FF), BF16)],
        compiler_params=_params(1),
        name="out_ffn2",
    )(x, yc, ya, wo, g, wg, wu, wd, gf)


def _mix_in(x, x_meta, ropek, ropeqt, g, win, cw, gq, wuqt, gkv, wuk, wuvt, rows, scale):
    bsz, seq, _ = x.shape
    n_t = seq // rows
    vt_chunk = min(VT_CHUNK, rows)
    per = rows // HALO
    last = seq // HALO - 1
    tile = lambda d: pl.BlockSpec((None, rows, d), lambda b, t: (b, t, 0))
    return pl.pallas_call(
        functools.partial(_mix_in_kernel, scale=scale),
        grid=(bsz, n_t),
        in_specs=[tile(D_MODEL),
                  pl.BlockSpec((None, HALO, D_MODEL), lambda b, t: (b, jnp.maximum(t * per - 1, 0), 0)),
                  pl.BlockSpec((None, HALO, D_MODEL), lambda b, t: (b, jnp.minimum((t + 1) * per, last), 0)),
                  _const_spec((N_META, D_MODEL)),
                  pl.BlockSpec((rows, HEAD_PAD), lambda b, t: (t, 0)),
                  pl.BlockSpec((2 * QK_ROPE, rows), lambda b, t: (0, t)),
                  _const_spec((1, D_MODEL)), _const_spec((D_MODEL, D_Z)), _const_spec((3, D_CONV)),
                  _const_spec((1, Q_LORA)), _const_spec((D_QT, Q_LORA)),
                  _const_spec((1, KV_LORA)), _const_spec((KV_LORA, D_QK)), _const_spec((D_V, KV_LORA))],
        out_specs=[tile(D_CONV), pl.BlockSpec((None, D_QK, rows), lambda b, t: (b, 0, t)),
                   pl.BlockSpec((None, N_HEADS, rows, HEAD_PAD), lambda b, t: (b, 0, t, 0)),
                   pl.BlockSpec((None, rows // vt_chunk, D_V, vt_chunk), lambda b, t: (b, t, 0, 0))],
        out_shape=[jax.ShapeDtypeStruct((bsz, seq, D_CONV), BF16),
                   jax.ShapeDtypeStruct((bsz, D_QK, seq), BF16),
                   jax.ShapeDtypeStruct((bsz, N_HEADS, seq, HEAD_PAD), BF16),
                   jax.ShapeDtypeStruct((bsz, seq // vt_chunk, D_V, vt_chunk), BF16)],
        compiler_params=_params(2),
        name="mix_in",
    )(x, x, x, x_meta, ropek, ropeqt, g, win, cw, gq, wuqt, gkv, wuk, wuvt)


def _attention(qt, k, vt, k_meta, vt_meta, q_block):
    bsz, _, seq = qt.shape
    _, n_kv, _, kv_block = vt.shape
    return pl.pallas_call(
        _attention_kernel,
        grid=(bsz, seq // q_block),
        in_specs=[pl.BlockSpec((None, D_QK, q_block), lambda b, t: (b, 0, t)),
                  pl.BlockSpec((None, N_HEADS, seq, HEAD_PAD), lambda b, t: (b, 0, 0, 0)),
                  pl.BlockSpec((None, n_kv, D_V, kv_block), lambda b, t: (b, 0, 0, 0)),
                  _const_spec((N_HEADS, N_META, HEAD_PAD)), _const_spec((D_V, N_META))],
        out_specs=pl.BlockSpec((None, q_block, D_V), lambda b, t: (b, t, 0)),
        out_shape=jax.ShapeDtypeStruct((bsz, seq, D_V), BF16),
        compiler_params=_params(2),
        name="attention",
    )(qt, k, vt, k_meta, vt_meta)


def _rot_half(w):
    half = QK_ROPE // 2
    return jnp.concatenate([-w[..., half:], w[..., :half]], axis=-1)


def _prepare_weights(w_in, w_uq, w_ukv):
    w_kr = w_in[:, 3 * D_CONV + Q_LORA + KV_LORA:]
    win = jnp.concatenate([w_in, _rot_half(w_kr), jnp.zeros((D_MODEL, HEAD_PAD - 2 * QK_ROPE), F32)], axis=1)
    wq = w_uq.reshape(Q_LORA, N_HEADS, QK_NOPE + QK_ROPE)
    zq = jnp.zeros((Q_LORA, N_HEADS, HEAD_PAD - QK_NOPE - QK_ROPE), F32)
    q_base = jnp.concatenate([wq, zq], axis=-1).reshape(Q_LORA, D_QK)
    q_rot = _rot_half(wq[..., QK_NOPE:]).reshape(Q_LORA, N_HEADS * QK_ROPE)
    wuqt = jnp.concatenate([q_base, q_rot], axis=1).T
    wkv = w_ukv.reshape(KV_LORA, N_HEADS, QK_NOPE + V_HEAD)
    wuk = jnp.concatenate([wkv[..., :QK_NOPE], jnp.zeros((KV_LORA, N_HEADS, HEAD_PAD - QK_NOPE), F32)],
                          axis=-1).reshape(KV_LORA, D_QK)
    wuvt = wkv[..., QK_NOPE:].reshape(KV_LORA, D_V).T
    return win.astype(BF16), wuqt.astype(BF16), wuk.astype(BF16), wuvt.astype(BF16)


def _rope_tables(n_pos):
    pos = jnp.arange(n_pos, dtype=F32)
    inv_freq = 1.0 / (ROPE_BASE ** (jnp.arange(0, QK_ROPE, 2, dtype=F32) / QK_ROPE))
    ang = pos[:, None] * inv_freq[None, :]
    cos, sin = jnp.cos(ang), jnp.sin(ang)
    table = jnp.concatenate([cos, cos, sin, sin], axis=1)
    return jnp.pad(table, ((0, 0), (0, HEAD_PAD - 2 * QK_ROPE))), table.T


def _pick(n, target):
    while n % target:
        target //= 2
    return target


def kernel(x_prompt, x_sample, meta_tokens, ffn1_norm, ffn1_w_gate, ffn1_w_up, ffn1_w_down, mix_norm, w_in, conv_w, q_norm, w_uq, kv_norm, w_ukv, w_out, ffn2_norm, ffn2_w_gate, ffn2_w_up, ffn2_w_down, final_norm):
    assert ffn1_norm.shape[0] == 1, "single layer: meta-token outputs are only consumed as keys/values"
    scale = (QK_NOPE + QK_ROPE) ** -0.5 * LOG2_E
    g1, g2, gm = ffn1_norm, ffn2_norm, mix_norm
    gq, gkv, gf = q_norm, kv_norm, final_norm.reshape(1, D_MODEL)
    wg1, wu1, wd1 = ffn1_w_gate[0].astype(BF16), ffn1_w_up[0].astype(BF16), ffn1_w_down[0].astype(BF16)
    wg2, wu2, wd2 = ffn2_w_gate[0].astype(BF16), ffn2_w_up[0].astype(BF16), ffn2_w_down[0].astype(BF16)
    wo = w_out[0].astype(BF16)
    win, wuqt, wuk, wuvt = _prepare_weights(w_in[0], w_uq[0], w_ukv[0])
    cw = conv_w[0]

    max_seq = max(x_prompt.shape[1], x_sample.shape[1])
    ropek, ropeqt = _rope_tables(N_META + max_seq)

    xm1 = _ffn1(meta_tokens, g1, wg1, wu1, wd1, N_META)
    _, _, k_meta, vt_meta = _mix_in(xm1[None], xm1, ropek[:N_META], ropeqt[:, :N_META], gm, win, cw, gq, wuqt, gkv, wuk, wuvt,
                                    N_META, scale)
    k_meta, vt_meta = k_meta[0], vt_meta[0, 0]

    def trunk(x):
        bsz, seq, _ = x.shape
        rows = _pick(seq, 1024)
        ffn_rows = _pick(bsz * seq, FFN_ROWS)
        x1 = _ffn1(x.reshape(bsz * seq, D_MODEL), g1, wg1, wu1, wd1, ffn_rows)
        yc, qt, k, vt = _mix_in(x1.reshape(bsz, seq, D_MODEL), xm1, ropek[N_META:N_META + seq], ropeqt[:, N_META:N_META + seq],
                                gm, win, cw, gq, wuqt, gkv, wuk, wuvt, rows, scale)
        ya = _attention(qt, k, vt, k_meta, vt_meta, _pick(seq, 512))
        y = _out_ffn2(x1, yc.reshape(bsz * seq, D_CONV), ya.reshape(bsz * seq, D_V), wo, g2, wg2, wu2, wd2, gf, ffn_rows)
        return y.reshape(bsz, seq, D_MODEL)

    return (trunk(x_prompt), trunk(x_sample))
```

```python
import functools

import jax
import jax.numpy as jnp
from jax import lax
from jax.experimental import pallas as pl
from jax.experimental.pallas import tpu as pltpu

D_MODEL = 1024
N_META = 16
D_CONV = 512
N_HEADS = 8
QK_NOPE = 64
QK_ROPE = 32
V_HEAD = 64
Q_LORA = 384
KV_LORA = 256
D_FF = 2816
ROPE_BASE = 10000.0
EPS = 1e-6
LOG2_E = 1.4426950408889634

HEAD_PAD = 128
D_QK = N_HEADS * HEAD_PAD
D_V = N_HEADS * V_HEAD
D_Z = 3 * D_CONV + Q_LORA + KV_LORA + HEAD_PAD
D_QT = D_QK + N_HEADS * QK_ROPE
FFN_ROWS = 1024
FF_CHUNK = 256
HALO = 8
BF16_ROWS = 16
MIX_ROW_PARTS = 4
VT_CHUNK = 512
VMEM_LIMIT = 56 * 1024 * 1024
ATTN_KV_BLOCK = 256
ATTN_Q_SUB = 512
ATTN_MAX_STEPS = 128
ATTN_LOOKAHEAD = 2

BF16 = jnp.bfloat16
F32 = jnp.float32
_NT = (((1,), (1,)), ((), ()))


def _dot(a, b):
    return jnp.dot(a, b, preferred_element_type=F32)


def _rms(x, g):
    ms = jnp.mean(x * x, axis=-1, keepdims=True)
    return x * lax.rsqrt(ms + EPS) * g


def _swiglu_half_step(x, g_ref, wg_ref, wu_ref, wd_ref, h_ref):
    xn = _rms(x, g_ref[...]).astype(BF16)
    for c in range(D_FF // FF_CHUNK):
        cols = slice(c * FF_CHUNK, (c + 1) * FF_CHUNK)
        gate = _dot(xn, wg_ref[:, cols])
        up = _dot(xn, wu_ref[:, cols])
        h_ref[:, cols] = (gate * jax.nn.sigmoid(gate) * up).astype(BF16)
    return x + 0.5 * _dot(h_ref[...], wd_ref[...])


def _ffn1_kernel(x_ref, g_ref, wg_ref, wu_ref, wd_ref, o_ref, h_ref):
    o_ref[...] = _swiglu_half_step(x_ref[...], g_ref, wg_ref, wu_ref, wd_ref, h_ref)


def _out_ffn2_kernel(x_ref, yc_ref, ya_ref, wo_ref, g_ref, wg_ref, wu_ref, wd_ref, gf_ref, o_ref, h_ref):
    x = x_ref[...] + _dot(yc_ref[...], wo_ref[:D_CONV, :]) + _dot(ya_ref[...], wo_ref[D_CONV:, :])
    x = _swiglu_half_step(x, g_ref, wg_ref, wu_ref, wd_ref, h_ref)
    o_ref[...] = _rms(x, gf_ref[...])


def _mix_in_kernel(x_ref, xl_ref, xr_ref, xm_ref, ropek_ref, ropeqt_ref, g_ref, win_ref, cw_ref,
                   gq_ref, wuqt_ref, gkv_ref, wuk_ref, wuvt_ref, yc_ref, qt_ref, k_ref, vt_ref, *, scale):
    t = pl.program_id(1)
    n_t = pl.num_programs(1)
    rows = x_ref.shape[0]
    g = g_ref[...]
    n_parts = MIX_ROW_PARTS if rows % (MIX_ROW_PARTS * HEAD_PAD) == 0 else 1
    parts = [slice(r, r + rows // n_parts) for r in range(0, rows, rows // n_parts)]
    z = jnp.concatenate([_dot(_rms(x_ref[r, :], g).astype(BF16), win_ref[...]) for r in parts], axis=0)
    b_gate = z[:, 0:D_CONV]
    u = z[:, D_CONV:2 * D_CONV] * z[:, 2 * D_CONV:3 * D_CONV]
    o_q = 3 * D_CONV
    o_kv = o_q + Q_LORA
    o_kr = o_kv + KV_LORA

    left = jnp.where(t == 0, xm_ref[N_META - HALO:, :], xl_ref[...])
    xh = jnp.concatenate([left, xr_ref[...]], axis=0)
    zh = _dot(_rms(xh, g).astype(BF16), win_ref[:, D_CONV:3 * D_CONV])
    uh = zh[:, :D_CONV] * zh[:, D_CONV:]
    u_before = uh[HALO - 1:HALO, :]
    u_after = jnp.where(t == n_t - 1, 0.0, uh[HALO:HALO + 1, :])
    row = lax.broadcasted_iota(jnp.int32, (rows, 1), 0)
    u_prev = jnp.where(row == 0, u_before, pltpu.roll(u, 1, 0))
    u_next = jnp.where(row == rows - 1, u_after, pltpu.roll(u, rows - 1, 0))
    cw = cw_ref[...]
    yc_ref[...] = (b_gate * (u_prev * cw[0:1, :] + u * cw[1:2, :] + u_next * cw[2:3, :])).astype(BF16)

    lane = lax.broadcasted_iota(jnp.int32, (1, HEAD_PAD), 1)
    rope_lanes = (lane >= QK_NOPE) & (lane < QK_NOPE + QK_ROPE)
    for r in parts:
        qn = _rms(z[r, o_q:o_kv], gq_ref[...]).astype(BF16)
        qqt = lax.dot_general(wuqt_ref[...], qn, _NT, preferred_element_type=F32)
        kvn = _rms(z[r, o_kv:o_kr], gkv_ref[...]).astype(BF16)
        kk = _dot(kvn, wuk_ref[...])
        vt = lax.dot_general(wuvt_ref[...], kvn, _NT, preferred_element_type=F32).astype(BF16)
        chunk = vt_ref.shape[-1]
        vt_ref[r.start // chunk, :, r.start % chunk:r.start % chunk + vt.shape[1]] = vt
        kr = z[r, o_kr:] * ropek_ref[r, :]
        kr = kr + pltpu.roll(kr, HEAD_PAD - QK_ROPE, 1)
        k_rope = jnp.where(rope_lanes, pltpu.roll(kr, QK_NOPE, 1), 0.0)
        cost = ropeqt_ref[:QK_ROPE, r]
        sint = ropeqt_ref[QK_ROPE:, r]
        for h in range(N_HEADS):
            cols = slice(h * HEAD_PAD, (h + 1) * HEAD_PAD)
            base = qqt[cols, :]
            rot = qqt[D_QK + h * QK_ROPE:D_QK + (h + 1) * QK_ROPE, :]
            q_rope = base[QK_NOPE:QK_NOPE + QK_ROPE] * cost + rot * sint
            qh = jnp.concatenate([base[:QK_NOPE], q_rope, base[QK_NOPE + QK_ROPE:]], axis=0)
            qt_ref[cols, r] = (qh * scale).astype(BF16)
            k_ref[h, r, :] = (kk[:, cols] + k_rope).astype(BF16)


def _attention_kernel(qt_ref, k_ref, vt_ref, km_ref, vmt_ref, o_ref):
    n_kv, _, kv_chunk = vt_ref.shape
    sub = min(ATTN_KV_BLOCK, kv_chunk)
    blocks = [(c, j) for c in range(n_kv) for j in range(kv_chunk // sub)]
    q_sub = min(ATTN_Q_SUB, qt_ref.shape[1])
    steps = [(qs, h, bl) for qs in range(qt_ref.shape[1] // q_sub) for h in range(N_HEADS)
             for bl in [((-1, 0), blocks[0])] + [(b,) for b in blocks[1:]]]

    def scores(step):
        qs, h, bl = step
        cols = slice(h * HEAD_PAD, (h + 1) * HEAD_PAD)
        out = []
        for c, j in bl:
            kc = km_ref[h] if c < 0 else k_ref[h, c * kv_chunk + j * sub:c * kv_chunk + (j + 1) * sub, :]
            out.append(_dot(kc, qt_ref[cols, qs * q_sub:(qs + 1) * q_sub]))
        s_max = functools.reduce(jnp.maximum, [jnp.max(s, axis=0, keepdims=True) for s in out])
        return out, s_max

    def weighted_values(h, block, p):
        c, j = block
        vrows = slice(h * V_HEAD, (h + 1) * V_HEAD)
        vtc = vmt_ref[vrows, :] if c < 0 else vt_ref[c, vrows, j * sub:(j + 1) * sub]
        vtc = jnp.concatenate([vtc, jnp.ones((BF16_ROWS, vtc.shape[1]), BF16)], axis=0)
        return _dot(vtc, p)

    ahead = [scores(st) for st in steps[:ATTN_LOOKAHEAD]]
    outs, m, acc = [], None, None
    for i, (qs, h, bl) in enumerate(steps):
        if i + ATTN_LOOKAHEAD < len(steps):
            ahead.append(scores(steps[i + ATTN_LOOKAHEAD]))
        ss, s_max = ahead.pop(0)
        first = bl[0][0] < 0
        m_new = s_max if first else jnp.maximum(m, s_max)
        pv = sum(weighted_values(h, b, jnp.exp2(s - m_new).astype(BF16)) for b, s in zip(bl, ss))
        acc = pv if first else jnp.exp2(m - m_new) * acc + pv
        m = m_new
        if bl[-1] == blocks[-1]:
            outs.append(acc[:V_HEAD] / acc[V_HEAD:V_HEAD + 1])
            if h == N_HEADS - 1:
                o_ref[qs * q_sub:(qs + 1) * q_sub, :] = jnp.concatenate(outs[-N_HEADS:], axis=0).T.astype(BF16)


def _const_spec(shape):
    return pl.BlockSpec(shape, lambda *_: (0,) * len(shape), pipeline_mode=pl.Buffered(1))


def _params(n_grid):
    return pltpu.CompilerParams(dimension_semantics=("arbitrary",) * n_grid, vmem_limit_bytes=VMEM_LIMIT)


def _ffn1(x, g, wg, wu, wd, rows):
    n = x.shape[0]
    return pl.pallas_call(
        _ffn1_kernel,
        grid=(n // rows,),
        in_specs=[pl.BlockSpec((rows, D_MODEL), lambda i: (i, 0)),
                  _const_spec((1, D_MODEL)), _const_spec((D_MODEL, D_FF)), _const_spec((D_MODEL, D_FF)),
                  _const_spec((D_FF, D_MODEL))],
        out_specs=pl.BlockSpec((rows, D_MODEL), lambda i: (i, 0)),
        out_shape=jax.ShapeDtypeStruct((n, D_MODEL), F32),
        scratch_shapes=[pltpu.VMEM((rows, D_FF), BF16)],
        compiler_params=_params(1),
        name="ffn1",
    )(x, g, wg, wu, wd)


def _out_ffn2(x, yc, ya, wo, g, wg, wu, wd, gf, rows):
    n = x.shape[0]
    return pl.pallas_call(
        _out_ffn2_kernel,
        grid=(n // rows,),
        in_specs=[pl.BlockSpec((rows, D_MODEL), lambda i: (i, 0)),
                  pl.BlockSpec((rows, D_CONV), lambda i: (i, 0)),
                  pl.BlockSpec((rows, D_V), lambda i: (i, 0)),
                  _const_spec((D_CONV + D_V, D_MODEL)),
                  _const_spec((1, D_MODEL)), _const_spec((D_MODEL, D_FF)), _const_spec((D_MODEL, D_FF)),
                  _const_spec((D_FF, D_MODEL)), _const_spec((1, D_MODEL))],
        out_specs=pl.BlockSpec((rows, D_MODEL), lambda i: (i, 0)),
        out_shape=jax.ShapeDtypeStruct((n, D_MODEL), F32),
        scratch_shapes=[pltpu.VMEM((rows, D_FF), BF16)],
        compiler_params=_params(1),
        name="out_ffn2",
    )(x, yc, ya, wo, g, wg, wu, wd, gf)


def _mix_in(x, x_meta, ropek, ropeqt, g, win, cw, gq, wuqt, gkv, wuk, wuvt, rows, scale):
    bsz, seq, _ = x.shape
    n_t = seq // rows
    vt_chunk = min(VT_CHUNK, rows)
    per = rows // HALO
    last = seq // HALO - 1
    tile = lambda d: pl.BlockSpec((None, rows, d), lambda b, t: (b, t, 0))
    return pl.pallas_call(
        functools.partial(_mix_in_kernel, scale=scale),
        grid=(bsz, n_t),
        in_specs=[tile(D_MODEL),
                  pl.BlockSpec((None, HALO, D_MODEL), lambda b, t: (b, jnp.maximum(t * per - 1, 0), 0)),
                  pl.BlockSpec((None, HALO, D_MODEL), lambda b, t: (b, jnp.minimum((t + 1) * per, last), 0)),
                  _const_spec((N_META, D_MODEL)),
                  pl.BlockSpec((rows, HEAD_PAD), lambda b, t: (t, 0)),
                  pl.BlockSpec((2 * QK_ROPE, rows), lambda b, t: (0, t)),
                  _const_spec((1, D_MODEL)), _const_spec((D_MODEL, D_Z)), _const_spec((3, D_CONV)),
                  _const_spec((1, Q_LORA)), _const_spec((D_QT, Q_LORA)),
                  _const_spec((1, KV_LORA)), _const_spec((KV_LORA, D_QK)), _const_spec((D_V, KV_LORA))],
        out_specs=[tile(D_CONV), pl.BlockSpec((None, D_QK, rows), lambda b, t: (b, 0, t)),
                   pl.BlockSpec((None, N_HEADS, rows, HEAD_PAD), lambda b, t: (b, 0, t, 0)),
                   pl.BlockSpec((None, rows // vt_chunk, D_V, vt_chunk), lambda b, t: (b, t, 0, 0))],
        out_shape=[jax.ShapeDtypeStruct((bsz, seq, D_CONV), BF16),
                   jax.ShapeDtypeStruct((bsz, D_QK, seq), BF16),
                   jax.ShapeDtypeStruct((bsz, N_HEADS, seq, HEAD_PAD), BF16),
                   jax.ShapeDtypeStruct((bsz, seq // vt_chunk, D_V, vt_chunk), BF16)],
        compiler_params=_params(2),
        name="mix_in",
    )(x, x, x, x_meta, ropek, ropeqt, g, win, cw, gq, wuqt, gkv, wuk, wuvt)


def _attention(qt, k, vt, k_meta, vt_meta, q_block):
    bsz, _, seq = qt.shape
    _, n_kv, _, kv_block = vt.shape
    return pl.pallas_call(
        _attention_kernel,
        grid=(bsz, seq // q_block),
        in_specs=[pl.BlockSpec((None, D_QK, q_block), lambda b, t: (b, 0, t)),
                  pl.BlockSpec((None, N_HEADS, seq, HEAD_PAD), lambda b, t: (b, 0, 0, 0)),
                  pl.BlockSpec((None, n_kv, D_V, kv_block), lambda b, t: (b, 0, 0, 0)),
                  _const_spec((N_HEADS, N_META, HEAD_PAD)), _const_spec((D_V, N_META))],
        out_specs=pl.BlockSpec((None, q_block, D_V), lambda b, t: (b, t, 0)),
        out_shape=jax.ShapeDtypeStruct((bsz, seq, D_V), BF16),
        compiler_params=_params(2),
        name="attention",
    )(qt, k, vt, k_meta, vt_meta)


def _rot_half(w):
    half = QK_ROPE // 2
    return jnp.concatenate([-w[..., half:], w[..., :half]], axis=-1)


def _prepare_weights(w_in, w_uq, w_ukv):
    w_kr = w_in[:, 3 * D_CONV + Q_LORA + KV_LORA:]
    win = jnp.concatenate([w_in, _rot_half(w_kr), jnp.zeros((D_MODEL, HEAD_PAD - 2 * QK_ROPE), F32)], axis=1)
    wq = w_uq.reshape(Q_LORA, N_HEADS, QK_NOPE + QK_ROPE)
    zq = jnp.zeros((Q_LORA, N_HEADS, HEAD_PAD - QK_NOPE - QK_ROPE), F32)
    q_base = jnp.concatenate([wq, zq], axis=-1).reshape(Q_LORA, D_QK)
    q_rot = _rot_half(wq[..., QK_NOPE:]).reshape(Q_LORA, N_HEADS * QK_ROPE)
    wuqt = jnp.concatenate([q_base, q_rot], axis=1).T
    wkv = w_ukv.reshape(KV_LORA, N_HEADS, QK_NOPE + V_HEAD)
    wuk = jnp.concatenate([wkv[..., :QK_NOPE], jnp.zeros((KV_LORA, N_HEADS, HEAD_PAD - QK_NOPE), F32)],
                          axis=-1).reshape(KV_LORA, D_QK)
    wuvt = wkv[..., QK_NOPE:].reshape(KV_LORA, D_V).T
    return win.astype(BF16), wuqt.astype(BF16), wuk.astype(BF16), wuvt.astype(BF16)


def _rope_tables(n_pos):
    pos = jnp.arange(n_pos, dtype=F32)
    inv_freq = 1.0 / (ROPE_BASE ** (jnp.arange(0, QK_ROPE, 2, dtype=F32) / QK_ROPE))
    ang = pos[:, None] * inv_freq[None, :]
    cos, sin = jnp.cos(ang), jnp.sin(ang)
    table = jnp.concatenate([cos, cos, sin, sin], axis=1)
    return jnp.pad(table, ((0, 0), (0, HEAD_PAD - 2 * QK_ROPE))), table.T


def _pick(n, target):
    while n % target:
        target //= 2
    return target


def kernel(x_prompt, x_sample, meta_tokens, ffn1_norm, ffn1_w_gate, ffn1_w_up, ffn1_w_down, mix_norm, w_in, conv_w, q_norm, w_uq, kv_norm, w_ukv, w_out, ffn2_norm, ffn2_w_gate, ffn2_w_up, ffn2_w_down, final_norm):
    assert ffn1_norm.shape[0] == 1, "single layer: meta-token outputs are only consumed as keys/values"
    scale = (QK_NOPE + QK_ROPE) ** -0.5 * LOG2_E
    g1, g2, gm = ffn1_norm, ffn2_norm, mix_norm
    gq, gkv, gf = q_norm, kv_norm, final_norm.reshape(1, D_MODEL)
    wg1, wu1, wd1 = ffn1_w_gate[0].astype(BF16), ffn1_w_up[0].astype(BF16), ffn1_w_down[0].astype(BF16)
    wg2, wu2, wd2 = ffn2_w_gate[0].astype(BF16), ffn2_w_up[0].astype(BF16), ffn2_w_down[0].astype(BF16)
    wo = w_out[0].astype(BF16)
    win, wuqt, wuk, wuvt = _prepare_weights(w_in[0], w_uq[0], w_ukv[0])
    cw = conv_w[0]

    max_seq = max(x_prompt.shape[1], x_sample.shape[1])
    ropek, ropeqt = _rope_tables(N_META + max_seq)

    xm1 = _ffn1(meta_tokens, g1, wg1, wu1, wd1, N_META)
    _, _, k_meta, vt_meta = _mix_in(xm1[None], xm1, ropek[:N_META], ropeqt[:, :N_META], gm, win, cw, gq, wuqt, gkv, wuk, wuvt,
                                    N_META, scale)
    k_meta, vt_meta = k_meta[0], vt_meta[0, 0]

    def trunk(x):
        bsz, seq, _ = x.shape
        rows = _pick(seq, 1024)
        ffn_rows = _pick(bsz * seq, FFN_ROWS)
        x1 = _ffn1(x.reshape(bsz * seq, D_MODEL), g1, wg1, wu1, wd1, ffn_rows)
        yc, qt, k, vt = _mix_in(x1.reshape(bsz, seq, D_MODEL), xm1, ropek[N_META:N_META + seq], ropeqt[:, N_META:N_META + seq],
                                gm, win, cw, gq, wuqt, gkv, wuk, wuvt, rows, scale)
        passes = max(1, ATTN_MAX_STEPS // (N_HEADS * (seq // ATTN_KV_BLOCK)))
        ya = _attention(qt, k, vt, k_meta, vt_meta, _pick(seq, ATTN_Q_SUB * passes))
        y = _out_ffn2(x1, yc.reshape(bsz * seq, D_CONV), ya.reshape(bsz * seq, D_V), wo, g2, wg2, wu2, wd2, gf, ffn_rows)
        return y.reshape(bsz, seq, D_MODEL)

    return (trunk(x_prompt), trunk(x_sample))
```
